```python
import math
import jax, jax.numpy as jnp
from jax import lax
import numpy as np

D_MODEL = 2048
BATCH = 2
SEQ = 4096
DEPTH = 4
DEC_BATCH = 8
DEC_SEQ = 4
PAST_LEN = 16384
PAGE_SIZE = 128

N_MIXERS = 2
N_DN_LAYERS = (DEPTH + 1) // 2
N_MB_LAYERS = DEPTH // 2
DN_QK_HEADS = 16
DN_V_HEADS = 32
DN_HEAD_K = 128
DN_HEAD_V = 128
DN_KEY_DIM = DN_QK_HEADS * DN_HEAD_K
DN_VAL_DIM = DN_V_HEADS * DN_HEAD_V
DN_CONV_DIM = 2 * DN_KEY_DIM + DN_VAL_DIM
DN_IN_DIM = DN_CONV_DIM + DN_VAL_DIM + 2 * DN_V_HEADS
CONV_W = 4
DN_CHUNK = 64
MB_HEADS = 16
MB_HEAD_DIM = D_MODEL // MB_HEADS
MB_BLOCK = 256
MB_TOPK = 3
MB_Q_CHUNK = 32
REL_BUCKETS = 32
REL_MAX_DIST = 4096
D_FF = 4 * D_MODEL
EPS = 1e-6
NEG_INF = -1e30

kernel_name = 'gdn_moba_hybrid_step'


def _rmsnorm(x, g):
    xf = x.astype(jnp.float32)
    y = xf * lax.rsqrt(jnp.mean(xf * xf, axis=-1, keepdims=True) + EPS) * g.astype(jnp.float32)
    return y.astype(x.dtype)


def _l2norm(x):
    return x * lax.rsqrt(jnp.sum(x * x, axis=-1, keepdims=True) + EPS)


def _t5_bucket(rel):
    n = jnp.maximum(rel, 0)
    max_exact = REL_BUCKETS // 2
    nf = jnp.maximum(n, max_exact).astype(jnp.float32)
    large = max_exact + (jnp.log(nf / max_exact) / math.log(REL_MAX_DIST / max_exact)
                         * (REL_BUCKETS - max_exact)).astype(jnp.int32)
    large = jnp.minimum(large, REL_BUCKETS - 1)
    return jnp.where(n < max_exact, n, large)


def _rel_bias(rel, head_idx, rel_bias):
    return rel_bias.astype(jnp.float32).T[head_idx, _t5_bucket(rel)]


def _mlp(h, w_up, w_down):
    return jnp.square(jax.nn.relu(h @ w_up)) @ w_down


def _gated_delta_rule(q, k, v, g, beta, s0):
    b, h, L, dk = q.shape
    dv = v.shape[-1]
    c = DN_CHUNK if L >= DN_CHUNK else L
    n = -(-L // c)
    pad = n * c - L

    def chunks(t):
        t = jnp.pad(t, [(0, 0), (0, 0), (0, pad)] + [(0, 0)] * (t.ndim - 3))
        return t.reshape((b, h, n, c) + t.shape[3:])

    q, k, v, g, beta = (chunks(t) for t in (q, k, v, g, beta))
    gc = jnp.cumsum(g, axis=-1)
    tril = jnp.tril(jnp.ones((c, c), bool))
    strict = jnp.tril(jnp.ones((c, c), bool), -1)
    decay = jnp.exp(jnp.where(tril, gc[..., :, None] - gc[..., None, :], NEG_INF))
    kb = k * beta[..., None]
    lmat = jnp.where(strict, jnp.einsum('bhnid,bhnjd->bhnij', kb, k) * decay, 0.0)
    eye = jnp.eye(c, dtype=jnp.float32)
    tinv = lax.linalg.triangular_solve(eye + lmat, jnp.broadcast_to(eye, lmat.shape),
                                       left_side=True, lower=True, unit_diagonal=True)
    u = jnp.einsum('bhnij,bhnje->bhnie', tinv, v * beta[..., None])
    w = jnp.einsum('bhnij,bhnjd->bhnid', tinv, kb * jnp.exp(gc)[..., None])
    a_intra = jnp.where(tril, jnp.einsum('bhnid,bhnjd->bhnij', q, k) * decay, 0.0)

    def step(s, xs):
        qi, ki, ui, wi, gi, ai = xs
        v_new = ui - jnp.einsum('bhcd,bhde->bhce', wi, s)
        o = (jnp.einsum('bhcd,bhde->bhce', qi * jnp.exp(gi)[..., None], s)
             + jnp.einsum('bhij,bhje->bhie', ai, v_new))
        g_last = gi[..., -1]
        s = (s * jnp.exp(g_last)[..., None, None]
             + jnp.einsum('bhcd,bhce->bhde', ki * jnp.exp(g_last[..., None] - gi)[..., None], v_new))
        return s, o

    xs = tuple(jnp.moveaxis(t, 2, 0) for t in (q, k, u, w, gc, a_intra))
    s, o = lax.scan(step, s0, xs)
    o = jnp.moveaxis(o, 0, 2).reshape(b, h, n * c, dv)[:, :, :L]
    return o, s


def _deltanet(h, conv_buf, s0, w_in, conv_w, a_log, dt_bias, norm_w, w_out):
    bsz, t, _ = h.shape
    f32 = jnp.float32
    proj = h @ w_in
    x_conv = proj[..., :DN_CONV_DIM]
    z = proj[..., DN_CONV_DIM:DN_CONV_DIM + DN_VAL_DIM]
    b_gate = proj[..., DN_CONV_DIM + DN_VAL_DIM:DN_CONV_DIM + DN_VAL_DIM + DN_V_HEADS]
    a_gate = proj[..., DN_CONV_DIM + DN_VAL_DIM + DN_V_HEADS:]
    xc = jnp.concatenate([conv_buf.astype(f32), x_conv.astype(f32)], axis=1)
    new_buf = xc[:, -(CONV_W - 1):]
    cw = conv_w.astype(f32)
    conv = xc[:, 0:t] * cw[0]
    for tap in range(1, CONV_W):
        conv = conv + xc[:, tap:tap + t] * cw[tap]
    conv = jax.nn.silu(conv)
    q = conv[..., :DN_KEY_DIM].reshape(bsz, t, DN_QK_HEADS, DN_HEAD_K)
    k = conv[..., DN_KEY_DIM:2 * DN_KEY_DIM].reshape(bsz, t, DN_QK_HEADS, DN_HEAD_K)
    v = conv[..., 2 * DN_KEY_DIM:].reshape(bsz, t, DN_V_HEADS, DN_HEAD_V)
    rep = DN_V_HEADS // DN_QK_HEADS
    q = jnp.repeat(_l2norm(q) * DN_HEAD_K ** -0.5, rep, axis=2)
    k = jnp.repeat(_l2norm(k), rep, axis=2)
    beta = jax.nn.sigmoid(b_gate.astype(f32))
    g = -jnp.exp(a_log.astype(f32)) * jax.nn.softplus(a_gate.astype(f32) + dt_bias.astype(f32))
    tr = lambda x: jnp.swapaxes(x, 1, 2)
    o, s = _gated_delta_rule(tr(q), tr(k), tr(v), tr(g), tr(beta), s0.astype(f32))
    o = tr(o)
    zf = z.astype(f32).reshape(bsz, t, DN_V_HEADS, DN_HEAD_V)
    o = o * lax.rsqrt(jnp.mean(o * o, axis=-1, keepdims=True) + EPS) * norm_w.astype(f32) * jax.nn.silu(zf)
    y = o.reshape(bsz, t, DN_VAL_DIM).astype(h.dtype) @ w_out
    return y, new_buf, s


def _moba_qkv(h, w_qkv, q_gain, k_gain):
    bsz, t, _ = h.shape
    qkv = (h @ w_qkv).reshape(bsz, t, 3, MB_HEADS, MB_HEAD_DIM)
    q = _rmsnorm(qkv[:, :, 0], q_gain)
    k = _rmsnorm(qkv[:, :, 1], k_gain)
    return q, k, qkv[:, :, 2]


def _moba_prompt(q, k, v, rel_bias):
    bsz, s, h, d = q.shape
    f32 = jnp.float32
    nblk = -(-s // MB_BLOCK)
    pad = nblk * MB_BLOCK - s

    def blocks(t):
        t = jnp.pad(t, ((0, 0), (0, pad), (0, 0), (0, 0)))
        return jnp.swapaxes(t, 1, 2).reshape(bsz, h, nblk, MB_BLOCK, d)

    kt, vt = blocks(k), blocks(v)
    kmean = jnp.sum(kt.astype(f32), axis=3) / MB_BLOCK
    nsel = min(MB_TOPK, nblk)
    nch = s // MB_Q_CHUNK
    q_chunks = jnp.swapaxes(q.reshape(bsz, nch, MB_Q_CHUNK, h, d), 0, 1)
    bi = jnp.arange(bsz)[:, None, None, None]
    hi = jnp.arange(h)[None, None, :, None]
    h_sel = jnp.arange(h)[None, None, :, None, None]
    h_own = jnp.arange(h)[None, :, None]
    r = jnp.arange(MB_BLOCK)
    scale = MB_HEAD_DIM ** -0.5

    def one_chunk(args):
        ci, qc = args
        start = ci * MB_Q_CHUNK
        cb = start // MB_BLOCK
        qpos = start + jnp.arange(MB_Q_CHUNK)
        s_blk = jnp.einsum('bqhd,bhnd->bqhn', qc.astype(f32), kmean)
        s_blk = jnp.where(jnp.arange(nblk) < cb, s_blk, NEG_INF)
        _, idx = lax.top_k(s_blk, nsel)
        ok = jnp.arange(nsel) < jnp.minimum(cb, MB_TOPK)
        kg = kt[bi, hi, idx]
        vg = vt[bi, hi, idx]
        kpos = idx[..., None] * MB_BLOCK + r
        s_sel = (jnp.einsum('bqhd,bqhnkd->bqhnk', qc, kg, preferred_element_type=f32) * scale
                 + _rel_bias(qpos[None, :, None, None, None] - kpos, h_sel, rel_bias))
        s_sel = jnp.where(ok[:, None], s_sel, NEG_INF)
        k_own = lax.dynamic_index_in_dim(kt, cb, axis=2, keepdims=False)
        v_own = lax.dynamic_index_in_dim(vt, cb, axis=2, keepdims=False)
        opos = cb * MB_BLOCK + r
        s_own = (jnp.einsum('bqhd,bhkd->bqhk', qc, k_own, preferred_element_type=f32) * scale
                 + _rel_bias(qpos[:, None, None] - opos[None, None, :], h_own, rel_bias))
        s_own = jnp.where(opos[None, None, :] <= qpos[:, None, None], s_own, NEG_INF)
        logits = jnp.concatenate([s_sel.reshape(bsz, MB_Q_CHUNK, h, nsel * MB_BLOCK), s_own], axis=-1)
        p = jax.nn.softmax(logits, axis=-1).astype(v.dtype)
        p_sel = p[..., :nsel * MB_BLOCK].reshape(bsz, MB_Q_CHUNK, h, nsel, MB_BLOCK)
        p_own = p[..., nsel * MB_BLOCK:]
        return jnp.einsum('bqhnk,bqhnkd->bqhd', p_sel, vg) + jnp.einsum('bqhk,bhkd->bqhd', p_own, v_own)

    o = lax.map(one_chunk, (jnp.arange(nch), q_chunks))
    return jnp.swapaxes(o, 0, 1).reshape(bsz, s, h, d)


def _moba_sample(q, k_new, v_new, cache_k, cache_v, layer, page_table, rel_bias):
    bsz, t, h, d = q.shape
    f32 = jnp.float32
    n_pages = page_table.shape[1]
    past = n_pages * PAGE_SIZE
    ppb = MB_BLOCK // PAGE_SIZE
    npn = -(-t // PAGE_SIZE)

    def pages(x):
        x = jnp.pad(x, ((0, 0), (0, npn * PAGE_SIZE - t), (0, 0), (0, 0)))
        return x.reshape(bsz, npn, PAGE_SIZE, h, d)

    kn, vn = pages(k_new), pages(v_new)
    nbt = -(-(past + t) // MB_BLOCK)
    page_sums = jnp.concatenate([jnp.sum(cache_k[layer, page_table].astype(f32), axis=2),
                                 jnp.sum(kn.astype(f32), axis=2)], axis=1)
    page_sums = jnp.pad(page_sums, ((0, 0), (0, nbt * ppb - n_pages - npn), (0, 0), (0, 0)))
    kmean = jnp.sum(page_sums.reshape(bsz, nbt, ppb, h, d), axis=2) / MB_BLOCK
    qpos = past + jnp.arange(t)
    cb = qpos // MB_BLOCK
    s_blk = jnp.einsum('bqhd,bnhd->bqhn', q.astype(f32), kmean)
    s_blk = jnp.where((jnp.arange(nbt)[None, :] < cb[:, None])[None, :, None, :], s_blk, NEG_INF)
    nsel = min(MB_TOPK, nbt)
    _, idx = lax.top_k(s_blk, nsel)
    ok = jnp.arange(nsel)[None, :] < jnp.minimum(cb, MB_TOPK)[:, None]
    idx = jnp.concatenate([idx, jnp.broadcast_to(cb[None, :, None, None], (bsz, t, h, 1)).astype(idx.dtype)], axis=-1)
    ok = jnp.concatenate([ok, jnp.ones((t, 1), bool)], axis=-1)
    ns = nsel + 1
    lp = idx[..., None] * ppb + jnp.arange(ppb)
    in_past = (lp < n_pages)[..., None, None]
    bi = jnp.arange(bsz)[:, None, None, None, None]
    hi = jnp.arange(h)[None, None, :, None, None]
    phys = page_table[bi, jnp.clip(lp, 0, n_pages - 1)]
    lpn = jnp.clip(lp - n_pages, 0, npn - 1)

    def gather(cache, new):
        g = jnp.where(in_past, cache[layer, phys, :, hi], new[bi, lpn, :, hi])
        return g.reshape(bsz, t, h, ns, MB_BLOCK, d)

    kg, vg = gather(cache_k, kn), gather(cache_v, vn)
    kpos = (lp[..., None] * PAGE_SIZE + jnp.arange(PAGE_SIZE)).reshape(bsz, t, h, ns, MB_BLOCK)
    qp = qpos[None, :, None, None, None]
    logits = (jnp.einsum('bqhd,bqhnkd->bqhnk', q, kg, preferred_element_type=f32) * MB_HEAD_DIM ** -0.5
              + _rel_bias(qp - kpos, hi, rel_bias))
    logits = jnp.where(ok[None, :, None, :, None] & (kpos <= qp), logits, NEG_INF)
    p = jax.nn.softmax(logits.reshape(bsz, t, h, ns * MB_BLOCK), axis=-1)
    p = p.reshape(bsz, t, h, ns, MB_BLOCK).astype(vg.dtype)
    return jnp.einsum('bqhnk,bqhnkd->bqhd', p, vg)


def setup_inputs(seed: int = 0) -> dict:
    key = jax.random.key(seed)
    ks = jax.random.split(key, 24)
    f32 = jnp.float32
    n_pages = PAST_LEN // PAGE_SIZE
    n_pool = (DEC_BATCH * n_pages * 5) // 4

    def nrm(k, shape, scale):
        return jax.random.normal(k, shape, f32) * scale

    x_prompt = nrm(ks[0], (BATCH, SEQ, D_MODEL), 1.0)
    x_sample = nrm(ks[1], (DEC_BATCH, DEC_SEQ, D_MODEL), 1.0)
    state_delta = nrm(ks[2], (N_DN_LAYERS, DEC_BATCH, DN_V_HEADS, DN_HEAD_K, DN_HEAD_V), 0.05)
    state_conv = nrm(ks[3], (N_DN_LAYERS, DEC_BATCH, CONV_W - 1, DN_CONV_DIM), 1.0)
    cache_k = nrm(ks[4], (N_MB_LAYERS, n_pool, PAGE_SIZE, MB_HEADS, MB_HEAD_DIM), 1.0)
    cache_v = nrm(ks[5], (N_MB_LAYERS, n_pool, PAGE_SIZE, MB_HEADS, MB_HEAD_DIM), 1.0)
    page_table = jax.random.permutation(ks[6], n_pool)[:DEC_BATCH * n_pages].reshape(DEC_BATCH, n_pages).astype(jnp.int32)
    rel_bias = nrm(ks[7], (REL_BUCKETS, MB_HEADS), 0.5)
    ln_mix = 1.0 + nrm(ks[8], (DEPTH, D_MODEL), 0.02)
    ln_mlp = 1.0 + nrm(ks[9], (DEPTH, D_MODEL), 0.02)
    dn_w_in = nrm(ks[10], (N_DN_LAYERS, D_MODEL, DN_IN_DIM), D_MODEL ** -0.5)
    dn_conv_w = nrm(ks[11], (N_DN_LAYERS, CONV_W, DN_CONV_DIM), CONV_W ** -0.5)
    dn_a_log = jnp.log(jax.random.uniform(ks[12], (N_DN_LAYERS, DN_V_HEADS), f32, 1.0, 16.0))
    dt = jnp.exp(jax.random.uniform(ks[13], (N_DN_LAYERS, DN_V_HEADS), f32, math.log(1e-3), math.log(1e-1)))
    dn_dt_bias = dt + jnp.log(-jnp.expm1(-dt))
    dn_norm_w = 1.0 + nrm(ks[14], (N_DN_LAYERS, DN_HEAD_V), 0.02)
    dn_w_out = nrm(ks[15], (N_DN_LAYERS, DN_VAL_DIM, D_MODEL), DN_VAL_DIM ** -0.5)
    mb_w_qkv = nrm(ks[16], (N_MB_LAYERS, D_MODEL, 3 * MB_HEADS * MB_HEAD_DIM), D_MODEL ** -0.5)
    mb_q_norm = 1.0 + nrm(ks[17], (N_MB_LAYERS, MB_HEAD_DIM), 0.02)
    mb_k_norm = 1.0 + nrm(ks[18], (N_MB_LAYERS, MB_HEAD_DIM), 0.02)
    mb_w_out = nrm(ks[19], (N_MB_LAYERS, MB_HEADS * MB_HEAD_DIM, D_MODEL), (MB_HEADS * MB_HEAD_DIM) ** -0.5)
    mlp_w_up = nrm(ks[20], (DEPTH, D_MODEL, D_FF), D_MODEL ** -0.5)
    mlp_w_down = nrm(ks[21], (DEPTH, D_FF, D_MODEL), D_FF ** -0.5)
    return {'x_prompt': x_prompt, 'x_sample': x_sample, 'state_delta': state_delta,
            'state_conv': state_conv, 'cache_k': cache_k, 'cache_v': cache_v,
            'page_table': page_table, 'rel_bias': rel_bias, 'ln_mix': ln_mix, 'ln_mlp': ln_mlp,
            'dn_w_in': dn_w_in, 'dn_conv_w': dn_conv_w, 'dn_a_log': dn_a_log,
            'dn_dt_bias': dn_dt_bias, 'dn_norm_w': dn_norm_w, 'dn_w_out': dn_w_out,
            'mb_w_qkv': mb_w_qkv, 'mb_q_norm': mb_q_norm, 'mb_k_norm': mb_k_norm,
            'mb_w_out': mb_w_out, 'mlp_w_up': mlp_w_up, 'mlp_w_down': mlp_w_down}


def reference(x_prompt, x_sample, state_delta, state_conv, cache_k, cache_v, page_table,
              rel_bias, ln_mix, ln_mlp, dn_w_in, dn_conv_w, dn_a_log, dn_dt_bias, dn_norm_w,
              dn_w_out, mb_w_qkv, mb_q_norm, mb_k_norm, mb_w_out, mlp_w_up, mlp_w_down):
    xp, xd = x_prompt, x_sample
    bp_ = xp.shape[0]
    sdp, scp, sds, scs = [], [], [], []
    kps, vps, kds, vds = [], [], [], []
    for i in range(DEPTH):
        j = i // N_MIXERS
        hp = _rmsnorm(xp, ln_mix[i])
        hd = _rmsnorm(xd, ln_mix[i])
        if i % N_MIXERS == 0:
            w = (dn_w_in[j], dn_conv_w[j], dn_a_log[j], dn_dt_bias[j], dn_norm_w[j], dn_w_out[j])
            buf0 = jnp.zeros((bp_, CONV_W - 1, DN_CONV_DIM), jnp.float32)
            s0 = jnp.zeros((bp_, DN_V_HEADS, DN_HEAD_K, DN_HEAD_V), jnp.float32)
            yp, buf_p, s_p = _deltanet(hp, buf0, s0, *w)
            yd, buf_d, s_d = _deltanet(hd, state_conv[j], state_delta[j], *w)
            sdp.append(s_p.astype(state_delta.dtype))
            scp.append(buf_p.astype(state_conv.dtype))
            sds.append(s_d.astype(state_delta.dtype))
            scs.append(buf_d.astype(state_conv.dtype))
        else:
            qp, kp, vp = _moba_qkv(hp, mb_w_qkv[j], mb_q_norm[j], mb_k_norm[j])
            qd, kd, vd = _moba_qkv(hd, mb_w_qkv[j], mb_q_norm[j], mb_k_norm[j])
            op = _moba_prompt(qp, kp, vp, rel_bias)
            od = _moba_sample(qd, kd, vd, cache_k, cache_v, j, page_table, rel_bias)
            yp = op.reshape(xp.shape[0], xp.shape[1], MB_HEADS * MB_HEAD_DIM) @ mb_w_out[j]
            yd = od.reshape(xd.shape[0], xd.shape[1], MB_HEADS * MB_HEAD_DIM) @ mb_w_out[j]
            kps.append(kp)
            vps.append(vp)
            kds.append(kd)
            vds.append(vd)
        xp = xp + yp.astype(xp.dtype)
        xd = xd + yd.astype(xd.dtype)
        xp = xp + _mlp(_rmsnorm(xp, ln_mlp[i]), mlp_w_up[i], mlp_w_down[i])
        xd = xd + _mlp(_rmsnorm(xd, ln_mlp[i]), mlp_w_up[i], mlp_w_down[i])
    new_state_delta_prompt = jnp.stack(sdp)
    new_state_conv_prompt = jnp.stack(scp)
    new_k_prompt = jnp.stack(kps)
    new_v_prompt = jnp.stack(vps)
    new_state_delta_sample = jnp.stack(sds)
    new_state_conv_sample = jnp.stack(scs)
    new_k_sample = jnp.stack(kds)
    new_v_sample = jnp.stack(vds)
    return (xp, xd, new_state_delta_prompt, new_state_conv_prompt, new_k_prompt, new_v_prompt,
            new_state_delta_sample, new_state_conv_sample, new_k_sample, new_v_sample)
```

```python
import functools
import math

import jax
import jax.numpy as jnp
from jax import lax
from jax.experimental import pallas as pl
from jax.experimental.pallas import tpu as pltpu

F32 = jnp.float32
BF16 = jnp.bfloat16

D_MODEL = 2048
DN_QK_HEADS = 16
DN_V_HEADS = 32
DN_HEAD = 128
DN_KEY_DIM = DN_QK_HEADS * DN_HEAD
DN_VAL_DIM = DN_V_HEADS * DN_HEAD
DN_CONV_DIM = 2 * DN_KEY_DIM + DN_VAL_DIM
CONV_W = 4
DN_CHUNK = 64
MB_HEADS = 16
MB_HEAD_DIM = 128
MB_BLOCK = 256
MB_TOPK = 3
PAGE_SIZE = 128
REL_BUCKETS = 32
REL_MAX_DIST = 4096
EPS = 1e-6
NEG_INF = -1e30

VMEM_LIMIT_BYTES = 56 * 1024 * 1024


def _cparams(*sem):
    return pltpu.CompilerParams(dimension_semantics=sem, vmem_limit_bytes=VMEM_LIMIT_BYTES)


def _pick_tile(n, pref):
    if n <= pref:
        return n
    t = pref
    while n % t:
        t //= 2
    return t


def _norm_matmul_kernel(x_ref, g_ref, w_ref, o_ref, hn_ref, *, act):
    @pl.when(pl.program_id(1) == 0)
    def _():
        x = x_ref[...]
        ms = jnp.mean(x * x, axis=-1, keepdims=True)
        hn_ref[...] = (x * lax.rsqrt(ms + EPS) * g_ref[...]).astype(hn_ref.dtype)

    y = jnp.dot(hn_ref[...], w_ref[...], preferred_element_type=F32)
    if act:
        y = jnp.square(jnp.maximum(y, 0.0))
    o_ref[...] = y.astype(o_ref.dtype)


def norm_matmul(x, g, w, *, act=False, out_dtype=F32, tm=1024, tn=1024):
    m, d = x.shape
    n = w.shape[1]
    tm = _pick_tile(m, tm)
    tn = _pick_tile(n, tn)
    return pl.pallas_call(
        functools.partial(_norm_matmul_kernel, act=act),
        out_shape=jax.ShapeDtypeStruct((m, n), out_dtype),
        grid=(m // tm, n // tn),
        in_specs=[pl.BlockSpec((tm, d), lambda i, j: (i, 0)),
                  pl.BlockSpec((1, d), lambda i, j: (0, 0)),
                  pl.BlockSpec((d, tn), lambda i, j: (0, j))],
        out_specs=pl.BlockSpec((tm, tn), lambda i, j: (i, j)),
        scratch_shapes=[pltpu.VMEM((tm, d), BF16)],
        compiler_params=_cparams("parallel", "arbitrary"),
        name="norm_matmul",
    )(x, g.reshape(1, d), w)


def _matmul_res_kernel(a_ref, w_ref, r_ref, o_ref, acc_ref, *, nk):
    k = pl.program_id(2)

    @pl.when(k == 0)
    def _():
        acc_ref[...] = jnp.zeros_like(acc_ref)

    acc_ref[...] += jnp.dot(a_ref[...], w_ref[...], preferred_element_type=F32)

    @pl.when(k == nk - 1)
    def _():
        o_ref[...] = r_ref[...] + acc_ref[...]


def matmul_residual(a, w, res, *, tm=1024, tn=1024, tk=2048):
    m, kdim = a.shape
    n = w.shape[1]
    tm = _pick_tile(m, tm)
    tn = _pick_tile(n, tn)
    tk = _pick_tile(kdim, tk)
    nk = kdim // tk
    return pl.pallas_call(
        functools.partial(_matmul_res_kernel, nk=nk),
        out_shape=jax.ShapeDtypeStruct((m, n), F32),
        grid=(m // tm, n // tn, nk),
        in_specs=[pl.BlockSpec((tm, tk), lambda i, j, k: (i, k)),
                  pl.BlockSpec((tk, tn), lambda i, j, k: (k, j)),
                  pl.BlockSpec((tm, tn), lambda i, j, k: (i, j))],
        out_specs=pl.BlockSpec((tm, tn), lambda i, j, k: (i, j)),
        scratch_shapes=[pltpu.VMEM((tm, tn), F32)],
        compiler_params=_cparams("parallel", "parallel", "arbitrary"),
        name="matmul_residual",
    )(a, w, res)


def _softplus(x):
    return jnp.maximum(x, 0.0) + jnp.log1p(jnp.exp(-jnp.abs(x)))


def _silu(x):
    return x * jax.nn.sigmoid(x)


def _bdot(a, b):
    return jnp.dot(a.astype(BF16), b.astype(BF16), preferred_element_type=F32)


def _bdot_nt(a, b):
    return lax.dot_general(a.astype(BF16), b.astype(BF16), (((1,), (1,)), ((), ())),
                           preferred_element_type=F32)


def _bdot_tn(a, b):
    return lax.dot_general(a.astype(BF16), b.astype(BF16), (((0,), (0,)), ((), ())),
                           preferred_element_type=F32)


def _unit_lower_inverse(lmat):
    c = lmat.shape[0]
    row = lax.broadcasted_iota(jnp.int32, (c, c), 0)
    col = lax.broadcasted_iota(jnp.int32, (c, c), 1)

    def same_block(size):
        shift = size.bit_length() - 1
        return (row >> shift) == (col >> shift)

    n1 = jnp.where(same_block(8), lmat, 0.0)
    n2 = _bdot(n1, n1)
    t = jnp.where(row == col, 1.0, 0.0) - n1
    t = t + _bdot(t, n2)
    t = t + _bdot(t, _bdot(n2, n2))
    size = 16
    while size <= c:
        cm = jnp.where(same_block(size) & jnp.logical_not(same_block(size // 2)), lmat, 0.0)
        t = t - _bdot(_bdot(t, cm), t)
        size *= 2
    return t


def _dn_core_kernel(alog_ref, dtb_ref,
                    q_ref, k_ref, v_ref, z_ref, gate_ref,
                    cwq_ref, cwk_ref, cwv_ref, nw_ref,
                    cbq_ref, cbk_ref, cbv_ref, s0_ref,
                    o_ref, sout_ref,
                    xq_ref, xk_ref, xv_ref, s_ref,
                    *, tb, t_valid, nt):
    hq = pl.program_id(1)
    t = pl.program_id(2)
    c = DN_CHUNK
    nchunk = tb // c
    pad = 8
    tail = CONV_W - 1

    @pl.when(t == 0)
    def _():
        xq_ref[pad - tail:pad, :] = cbq_ref[0]
        xk_ref[pad - tail:pad, :] = cbk_ref[0]
        xv_ref[pad - tail:pad, :] = cbv_ref[0]
        s_ref[...] = s0_ref[0]

    xq_ref[pad:pad + tb, :] = q_ref[0]
    xk_ref[pad:pad + tb, :] = k_ref[0]
    xv_ref[pad:pad + tb, :] = v_ref[0]

    def conv(x_ref, cw_ref):
        acc = x_ref[pad - tail:pad - tail + tb, :] * cw_ref[0:1, :]
        for tap in range(1, CONV_W):
            acc = acc + x_ref[pad - tail + tap:pad - tail + tap + tb, :] * cw_ref[tap:tap + 1, :]
        return _silu(acc)

    q = conv(xq_ref, cwq_ref)
    k = conv(xk_ref, cwk_ref)
    v2 = conv(xv_ref, cwv_ref)

    xq_ref[pad - tail:pad, :] = xq_ref[pad + tb - tail:pad + tb, :]
    xk_ref[pad - tail:pad, :] = xk_ref[pad + tb - tail:pad + tb, :]
    xv_ref[pad - tail:pad, :] = xv_ref[pad + tb - tail:pad + tb, :]

    q = q * lax.rsqrt(jnp.sum(q * q, axis=-1, keepdims=True) + EPS) * (DN_HEAD ** -0.5)
    k = k * lax.rsqrt(jnp.sum(k * k, axis=-1, keepdims=True) + EPS)

    row = lax.broadcasted_iota(jnp.int32, (c, c), 0)
    col = lax.broadcasted_iota(jnp.int32, (c, c), 1)
    eye = row == col
    tril = row >= col
    strict = row > col
    lane_t = lax.broadcasted_iota(jnp.int32, (1, c), 1)

    for ci in range(nchunk):
        r0 = ci * c
        qc = q[r0:r0 + c]
        kc = k[r0:r0 + c]
        qk = _bdot_nt(jnp.concatenate([qc, kc], axis=0), kc)
        qkt, kkt = qk[:c], qk[c:]
        gates = gate_ref[0, 0, ci]
        valid = (t * tb + r0 + lane_t) < t_valid
        for r in range(2):
            head = 2 * hq + r
            beta_row = jnp.where(valid, jax.nn.sigmoid(gates[r:r + 1, :]), 0.0)
            a_scale = -jnp.exp(jnp.full((1, c), alog_ref[head], F32))
            g_row = jnp.where(valid, a_scale * _softplus(gates[2 + r:3 + r, :] + dtb_ref[head]), 0.0)
            gc_col = jnp.sum(jnp.where(tril, g_row, 0.0), axis=1, keepdims=True)
            gc_row = jnp.sum(jnp.where(eye, gc_col, 0.0), axis=0, keepdims=True)
            beta_col = jnp.sum(jnp.where(eye, beta_row, 0.0), axis=1, keepdims=True)
            decay = jnp.exp(jnp.where(tril, gc_col - gc_row, NEG_INF))
            lmat = jnp.where(strict, kkt * beta_col * decay, 0.0)
            a_intra = jnp.where(tril, qkt * decay, 0.0)
            tinv = _unit_lower_inverse(lmat)

            vc = v2[r0:r0 + c, r * DN_HEAD:(r + 1) * DN_HEAD]
            egc = jnp.exp(gc_col)
            rhs = jnp.concatenate([vc * beta_col, kc * (beta_col * egc)], axis=1)
            uw = _bdot(tinv, rhs)
            u, w = uw[:, :DN_HEAD], uw[:, DN_HEAD:]

            s = s_ref[r]
            ws_qs = _bdot(jnp.concatenate([w, qc * egc], axis=0), s)
            v_new = u - ws_qs[:c]
            o = ws_qs[c:] + _bdot(a_intra, v_new)
            g_last = gc_col[c - 1:c, :]
            s_ref[r] = s * jnp.exp(g_last) + _bdot_tn(kc * jnp.exp(g_last - gc_col), v_new)

            zc = z_ref[0, r0:r0 + c, r * DN_HEAD:(r + 1) * DN_HEAD]
            o = o * lax.rsqrt(jnp.mean(o * o, axis=-1, keepdims=True) + EPS) * nw_ref[...] * _silu(zc)
            o_ref[0, r0:r0 + c, r * DN_HEAD:(r + 1) * DN_HEAD] = o.astype(o_ref.dtype)

    @pl.when(t == nt - 1)
    def _():
        sout_ref[0] = s_ref[...]


def dn_core(proj, gates_t, conv_w, a_log, dt_bias, norm_w, conv_buf, s0, *, t_valid, tb):
    b, t, _ = proj.shape
    nt = t // tb
    nq = DN_QK_HEADS
    kern = functools.partial(_dn_core_kernel, tb=tb, t_valid=t_valid, nt=nt)
    smem = pl.BlockSpec(memory_space=pltpu.SMEM)
    in_specs = [
        smem, smem,
        pl.BlockSpec((1, tb, 128), lambda bi, h, ti: (bi, ti, h)),
        pl.BlockSpec((1, tb, 128), lambda bi, h, ti: (bi, ti, nq + h)),
        pl.BlockSpec((1, tb, 256), lambda bi, h, ti: (bi, ti, nq + h)),
        pl.BlockSpec((1, tb, 256), lambda bi, h, ti: (bi, ti, 2 * nq + h)),
        pl.BlockSpec((1, 1, tb // DN_CHUNK, 4, DN_CHUNK), lambda bi, h, ti: (bi, h, ti, 0, 0)),
        pl.BlockSpec((CONV_W, 128), lambda bi, h, ti: (0, h)),
        pl.BlockSpec((CONV_W, 128), lambda bi, h, ti: (0, nq + h)),
        pl.BlockSpec((CONV_W, 256), lambda bi, h, ti: (0, nq + h)),
        pl.BlockSpec((1, 128), lambda bi, h, ti: (0, 0)),
        pl.BlockSpec((1, CONV_W - 1, 128), lambda bi, h, ti: (bi, 0, h)),
        pl.BlockSpec((1, CONV_W - 1, 128), lambda bi, h, ti: (bi, 0, nq + h)),
        pl.BlockSpec((1, CONV_W - 1, 256), lambda bi, h, ti: (bi, 0, nq + h)),
        pl.BlockSpec((1, 2, 128, 128), lambda bi, h, ti: (bi, h, 0, 0)),
    ]
    out_specs = [
        pl.BlockSpec((1, tb, 256), lambda bi, h, ti: (bi, ti, h)),
        pl.BlockSpec((1, 2, 128, 128), lambda bi, h, ti: (bi, h, 0, 0)),
    ]
    return pl.pallas_call(
        kern,
        out_shape=[jax.ShapeDtypeStruct((b, t, DN_VAL_DIM), BF16),
                   jax.ShapeDtypeStruct((b, DN_V_HEADS, 128, 128), F32)],
        grid=(b, nq, nt),
        in_specs=in_specs,
        out_specs=out_specs,
        scratch_shapes=[pltpu.VMEM((tb + 8, 128), F32), pltpu.VMEM((tb + 8, 128), F32),
                        pltpu.VMEM((tb + 8, 256), F32), pltpu.VMEM((2, 128, 128), F32)],
        compiler_params=_cparams("parallel", "parallel", "arbitrary"),
        name="dn_core",
    )(a_log, dt_bias, proj, proj, proj, proj, gates_t,
      conv_w, conv_w, conv_w, norm_w.reshape(1, 128),
      conv_buf, conv_buf, conv_buf, s0)


def deltanet_layer(x, ln_g, w_main, w_gate, conv_w, a_log, dt_bias, norm_w, w_out, conv_buf, s0):
    b, t, d = x.shape
    x2 = x.reshape(b * t, d)
    proj = norm_matmul(x2, ln_g, w_main).reshape(b, t, -1)
    gates = norm_matmul(x2, ln_g, w_gate).reshape(b, t, -1)[:, :, :2 * DN_V_HEADS]
    new_buf = jnp.concatenate([conv_buf, proj[:, :, :DN_CONV_DIM]], axis=1)[:, -(CONV_W - 1):]
    tp = -(-t // DN_CHUNK) * DN_CHUNK
    if tp != t:
        proj = jnp.pad(proj, ((0, 0), (0, tp - t), (0, 0)))
        gates = jnp.pad(gates, ((0, 0), (0, tp - t), (0, 0)))
    g5 = gates.reshape(b, tp // DN_CHUNK, DN_CHUNK, 2, DN_QK_HEADS, 2)
    gates_t = jnp.transpose(g5, (0, 4, 1, 3, 5, 2)).reshape(b, DN_QK_HEADS, tp // DN_CHUNK, 4, DN_CHUNK)
    tb = _pick_tile(tp, 256)
    o, s_new = dn_core(proj, gates_t, conv_w, a_log, dt_bias, norm_w, conv_buf, s0, t_valid=t, tb=tb)
    o = o[:, :t].reshape(b * t, DN_VAL_DIM)
    y = matmul_residual(o, w_out, x2)
    return y.reshape(b, t, d), new_buf, s_new


def mlp_layer(x, ln_g, w_up, w_down):
    b, t, d = x.shape
    x2 = x.reshape(b * t, d)
    hid = norm_matmul(x2, ln_g, w_up, act=True, out_dtype=BF16)
    return matmul_residual(hid, w_down, x2).reshape(b, t, d)


def _rel_bias_table_kernel(rbt_ref, o_ref, *, r0, ltab):
    rel = (r0 + pl.program_id(0)) - lax.broadcasted_iota(jnp.int32, (1, ltab), 1)
    n = jnp.maximum(rel, 0)
    max_exact = REL_BUCKETS // 2
    nf = jnp.maximum(n, max_exact).astype(F32)
    large = max_exact + (jnp.log(nf / max_exact) / math.log(REL_MAX_DIST / max_exact)
                         * (REL_BUCKETS - max_exact)).astype(jnp.int32)
    large = jnp.minimum(large, REL_BUCKETS - 1)
    bucket = jnp.where(n < max_exact, n, large)
    rbt = rbt_ref[...]
    out = jnp.zeros((MB_HEADS, ltab), F32)
    for bkt in range(REL_BUCKETS):
        out = jnp.where(bucket == bkt, rbt[:, bkt:bkt + 1], out)
    o_ref[0] = out


def rel_bias_table(rel_bias, *, r0, nrow, ltab):
    return pl.pallas_call(
        functools.partial(_rel_bias_table_kernel, r0=r0, ltab=ltab),
        out_shape=jax.ShapeDtypeStruct((nrow, MB_HEADS, ltab), F32),
        grid=(nrow,),
        in_specs=[pl.BlockSpec((MB_HEADS, REL_BUCKETS), lambda i: (0, 0))],
        out_specs=pl.BlockSpec((1, MB_HEADS, ltab), lambda i: (i, 0, 0)),
        compiler_params=_cparams("arbitrary"),
        name="rel_bias_table",
    )(rel_bias.T)


def _qk_norm_kernel(q_ref, k_ref, qg_ref, kg_ref, qn_ref, kn_ref, km_ref):
    qg = qg_ref[...]
    kg = kg_ref[...]
    for h in range(MB_HEADS):
        sl = slice(h * MB_HEAD_DIM, (h + 1) * MB_HEAD_DIM)
        qh = q_ref[0, :, sl]
        kh = k_ref[0, :, sl]
        qn_ref[0, :, sl] = qh * lax.rsqrt(jnp.mean(qh * qh, axis=-1, keepdims=True) + EPS) * qg
        kn = kh * lax.rsqrt(jnp.mean(kh * kh, axis=-1, keepdims=True) + EPS) * kg
        kn_ref[0, :, sl] = kn
        km_ref[0, 0, :, sl] = jnp.sum(kn, axis=0, keepdims=True) / MB_BLOCK


def qk_norm(qkv, q_gain, k_gain, *, rows):
    b, t, _ = qkv.shape
    nb = t // rows
    d = MB_HEADS * MB_HEAD_DIM
    return pl.pallas_call(
        _qk_norm_kernel,
        out_shape=[jax.ShapeDtypeStruct((b, t, d), F32), jax.ShapeDtypeStruct((b, t, d), F32),
                   jax.ShapeDtypeStruct((b, nb, 1, d), F32)],
        grid=(b, nb),
        in_specs=[pl.BlockSpec((1, rows, d), lambda bi, i: (bi, i, 0)),
                  pl.BlockSpec((1, rows, d), lambda bi, i: (bi, i, 1)),
                  pl.BlockSpec((1, MB_HEAD_DIM), lambda bi, i: (0, 0)),
                  pl.BlockSpec((1, MB_HEAD_DIM), lambda bi, i: (0, 0))],
        out_specs=[pl.BlockSpec((1, rows, d), lambda bi, i: (bi, i, 0)),
                   pl.BlockSpec((1, rows, d), lambda bi, i: (bi, i, 0)),
                   pl.BlockSpec((1, 1, 1, d), lambda bi, i: (bi, i, 0, 0))],
        compiler_params=_cparams("parallel", "parallel"),
        name="qk_norm",
    )(qkv, qkv, q_gain.reshape(1, -1), k_gain.reshape(1, -1))


def _block_scores(q, km):
    nb = km.shape[0]
    km = jnp.concatenate([km, jnp.zeros((128 - nb, km.shape[1]), F32)], axis=0)
    return _bdot_nt(q, km)


def _block_ranks(scores, ncand, nb):
    lane = lax.broadcasted_iota(jnp.int32, scores.shape, 1)
    sm = jnp.where(lane < ncand, scores, NEG_INF)
    cnt = jnp.zeros(scores.shape, jnp.int32)
    for m in range(nb):
        col = sm[:, m:m + 1]
        better = (col > sm) | ((col == sm) & (lane > m))
        cnt = cnt + jnp.where(better & (m < ncand), 1, 0)
    return cnt


def _moba_prompt_kernel(tbl_ref, q_ref, k_ref, v_ref, km_ref, o_ref, bias_ref, wide_ref, *, nblk):
    b = pl.program_id(1)
    qi = pl.program_id(2)
    blk = MB_BLOCK
    scale = MB_HEAD_DIM ** -0.5

    @pl.when(b == 0)
    def _():
        start = pl.multiple_of((nblk - 1 - qi) * blk, blk)
        u = tbl_ref[0, :, pl.ds(start, 2 * blk)]
        wide_ref[...] = jnp.broadcast_to(u, (blk, 2 * blk))
        tile = pltpu.roll(wide_ref[...], blk + 1, 1, stride=1, stride_axis=0)
        bias_ref[qi] = tile[:, :blk]

    q = q_ref[0]
    lane = lax.broadcasted_iota(jnp.int32, (blk, 128), 1)
    rank = _block_ranks(_block_scores(q, km_ref[0, :, 0, :]), qi, nblk)
    sel = jnp.where((lane < qi) & (rank < MB_TOPK), 1.0, 0.0)

    qb = q.astype(BF16)
    row = lax.broadcasted_iota(jnp.int32, (blk, blk), 0)
    col = lax.broadcasted_iota(jnp.int32, (blk, blk), 1)
    own0 = pl.multiple_of(qi * blk, blk)
    s = _bdot_nt(qb, k_ref[0, pl.ds(own0, blk), :]) * scale + bias_ref[0]
    s = jnp.where(row >= col, s, NEG_INF)
    m0 = jnp.max(s, axis=1, keepdims=True)
    p = jnp.exp(s - m0)
    l0 = jnp.sum(p, axis=1, keepdims=True)
    acc0 = _bdot(p, v_ref[0, pl.ds(own0, blk), :])

    def body(n, carry):
        m, l, acc = carry
        r0 = pl.multiple_of(n * blk, blk)
        picked = jnp.sum(jnp.where(lane == n, sel, 0.0), axis=1, keepdims=True) > 0.0
        s = _bdot_nt(qb, k_ref[0, pl.ds(r0, blk), :]) * scale + bias_ref[qi - n]
        s = jnp.where(picked, s, NEG_INF)
        m_new = jnp.maximum(m, jnp.max(s, axis=1, keepdims=True))
        alpha = jnp.exp(m - m_new)
        p = jnp.where(picked, jnp.exp(s - m_new), 0.0)
        l = l * alpha + jnp.sum(p, axis=1, keepdims=True)
        acc = acc * alpha + _bdot(p, v_ref[0, pl.ds(r0, blk), :])
        return m_new, l, acc

    m, l, acc = lax.fori_loop(0, qi, body, (m0, l0, acc0))
    o_ref[0] = (acc / l).astype(o_ref.dtype)


def moba_prompt_attention(qn, kn, qkv, kmean, tbl):
    b, s, d = qn.shape
    nblk = s // MB_BLOCK
    hd = MB_HEAD_DIM
    return pl.pallas_call(
        functools.partial(_moba_prompt_kernel, nblk=nblk),
        out_shape=jax.ShapeDtypeStruct((b, s, d), BF16),
        grid=(MB_HEADS, b, nblk),
        in_specs=[pl.BlockSpec((1, 1, s + MB_BLOCK), lambda h, bi, i: (h, 0, 0)),
                  pl.BlockSpec((1, MB_BLOCK, hd), lambda h, bi, i: (bi, i, h)),
                  pl.BlockSpec((1, s, hd), lambda h, bi, i: (bi, 0, h)),
                  pl.BlockSpec((1, s, hd), lambda h, bi, i: (bi, 0, 2 * MB_HEADS + h)),
                  pl.BlockSpec((1, nblk, 1, hd), lambda h, bi, i: (bi, 0, 0, h))],
        out_specs=pl.BlockSpec((1, MB_BLOCK, hd), lambda h, bi, i: (bi, i, h)),
        scratch_shapes=[pltpu.VMEM((nblk, MB_BLOCK, MB_BLOCK), F32),
                        pltpu.VMEM((MB_BLOCK, 2 * MB_BLOCK), F32)],
        compiler_params=_cparams("parallel", "arbitrary", "arbitrary"),
        name="moba_prompt_attention",
    )(tbl, qn, kn, qkv, kmean)


def moba_prompt_layer(x, ln_g, w_qkv, q_gain, k_gain, w_out, tbl):
    b, s, d = x.shape
    assert s % MB_BLOCK == 0
    x2 = x.reshape(b * s, d)
    qkv = norm_matmul(x2, ln_g, w_qkv).reshape(b, s, -1)
    qn, kn, kmean = qk_norm(qkv, q_gain, k_gain, rows=MB_BLOCK)
    o = moba_prompt_attention(qn, kn, qkv, kmean, tbl)
    y = matmul_residual(o.reshape(b * s, d), w_out, x2).reshape(b, s, d)
    new_k = kn.reshape(b, s, MB_HEADS, MB_HEAD_DIM)
    new_v = qkv[:, :, 2 * d:].reshape(b, s, MB_HEADS, MB_HEAD_DIM)
    return y, new_k, new_v


def _page_block_mean_kernel(pt_ref, a_ref, b_ref, o_ref):
    del pt_ref
    o_ref[0, 0] = (jnp.sum(a_ref[0, 0], axis=0, keepdims=True)
                   + jnp.sum(b_ref[0, 0], axis=0, keepdims=True)) / MB_BLOCK


def page_block_means(cache_k4, layer, pt_flat, bsz, n_pages):
    nb = n_pages // 2
    d = cache_k4.shape[-1]

    def page_spec(off):
        return pl.BlockSpec((1, 1, PAGE_SIZE, d),
                            lambda bi, n, pt: (layer, pt[bi * n_pages + 2 * n + off], 0, 0))

    return pl.pallas_call(
        _page_block_mean_kernel,
        out_shape=jax.ShapeDtypeStruct((bsz, nb, 1, d), F32),
        grid_spec=pltpu.PrefetchScalarGridSpec(
            num_scalar_prefetch=1, grid=(bsz, nb),
            in_specs=[page_spec(0), page_spec(1)],
            out_specs=pl.BlockSpec((1, 1, 1, d), lambda bi, n, pt: (bi, n, 0, 0))),
        compiler_params=_cparams("parallel", "parallel"),
        name="page_block_means",
    )(pt_flat, cache_k4, cache_k4)


def _sample_select_kernel(q_ref, km_ref, o_ref, *, past, nb):
    lane = lax.broadcasted_iota(jnp.int32, (8, 128), 1)
    ncand = (past + lax.broadcasted_iota(jnp.int32, (8, 1), 0)) // MB_BLOCK
    cand = lane < ncand
    t = q_ref.shape[1]
    for h in range(MB_HEADS):
        sl = slice(h * MB_HEAD_DIM, (h + 1) * MB_HEAD_DIM)
        q = jnp.concatenate([q_ref[0, :, sl], jnp.zeros((8 - t, MB_HEAD_DIM), F32)], axis=0)
        rank = _block_ranks(_block_scores(q, km_ref[0, :, 0, sl]), ncand, nb)
        out = jnp.zeros((8, 128), jnp.int32)
        for r in range(MB_TOPK):
            idx = jnp.sum(jnp.where(cand & (rank == r), lane, 0), axis=1, keepdims=True)
            out = jnp.where(lane == r, idx, out)
        o_ref[0, h] = out


def sample_select(qn, kmean, past):
    b, t, d = qn.shape
    nb = kmean.shape[1]
    assert t <= 8 and nb <= 128
    return pl.pallas_call(
        functools.partial(_sample_select_kernel, past=past, nb=nb),
        out_shape=jax.ShapeDtypeStruct((b, MB_HEADS, 8, 128), jnp.int32),
        grid=(b,),
        in_specs=[pl.BlockSpec((1, t, d), lambda bi: (bi, 0, 0)),
                  pl.BlockSpec((1, nb, 1, d), lambda bi: (bi, 0, 0, 0))],
        out_specs=pl.BlockSpec((1, MB_HEADS, 8, 128), lambda bi: (bi, 0, 0, 0)),
        compiler_params=_cparams("parallel"),
        name="sample_select",
    )(qn, kmean)


def _sample_attn_kernel(idx_ref, pt_ref, q_ref, kn_ref, vn_ref, kp_ref, vp_ref, bias_ref, bown_ref,
                        o_ref, m_ref, l_ref, acc_ref, *, past, nq):
    del idx_ref, pt_ref
    s = pl.program_id(2)
    per_q = MB_TOPK * 2
    t = s // per_q
    r = (s % per_q) // 2
    scale = MB_HEAD_DIM ** -0.5
    q8 = jnp.broadcast_to(q_ref[0, pl.ds(t, 1), :], (8, MB_HEAD_DIM))
    lane = lax.broadcasted_iota(jnp.int32, (1, PAGE_SIZE), 1)

    @pl.when(s % per_q == 0)
    def _():
        zpad = jnp.zeros((PAGE_SIZE - nq, MB_HEAD_DIM), F32)
        kown = jnp.concatenate([kn_ref[0], zpad], axis=0)
        vown = jnp.concatenate([vn_ref[0], zpad], axis=0)
        sc = _bdot_nt(q8, kown)[0:1] * scale + bown_ref[0, 0]
        sc = jnp.where(lane <= t, sc, NEG_INF)
        m0 = jnp.max(sc, axis=1, keepdims=True)
        p = jnp.exp(sc - m0)
        m_ref[...] = m0
        l_ref[...] = jnp.sum(p, axis=1, keepdims=True)
        acc_ref[...] = _bdot(jnp.broadcast_to(p, (8, PAGE_SIZE)), vown)[0:1]

    ok = r < jnp.minimum((past + t) // MB_BLOCK, MB_TOPK)
    sc = _bdot_nt(q8, kp_ref[0, 0])[0:1] * scale + bias_ref[0, 0]
    sc = jnp.where(ok, sc, NEG_INF)
    m_old = m_ref[...]
    m_new = jnp.maximum(m_old, jnp.max(sc, axis=1, keepdims=True))
    alpha = jnp.exp(m_old - m_new)
    p = jnp.where(ok, jnp.exp(sc - m_new), 0.0)
    m_ref[...] = m_new
    l_ref[...] = l_ref[...] * alpha + jnp.sum(p, axis=1, keepdims=True)
    acc_ref[...] = acc_ref[...] * alpha + _bdot(jnp.broadcast_to(p, (8, PAGE_SIZE)), vp_ref[0, 0])[0:1]

    @pl.when(s % per_q == per_q - 1)
    def _():
        o_ref[0, pl.ds(t, 1), :] = (acc_ref[...] / l_ref[...]).astype(o_ref.dtype)


def sample_attention(qn, kn, qkv, cache_k4, cache_v4, layer, idx_flat, pt_flat, tbl, n_pages):
    b, t, d = qn.shape
    hd = MB_HEAD_DIM
    past = n_pages * PAGE_SIZE
    per_q = MB_TOPK * 2

    def page_of(bi, h, s, idx, pt):
        tq = s // per_q
        r = (s % per_q) // 2
        return idx[((bi * MB_HEADS + h) * t + tq) * MB_TOPK + r] * 2 + s % 2

    def cache_spec():
        return pl.BlockSpec((1, 1, PAGE_SIZE, hd),
                            lambda bi, h, s, idx, pt: (layer, pt[bi * n_pages + page_of(bi, h, s, idx, pt)], 0, h))

    in_specs = [
        pl.BlockSpec((1, t, hd), lambda bi, h, s, idx, pt: (bi, 0, h)),
        pl.BlockSpec((1, t, hd), lambda bi, h, s, idx, pt: (bi, 0, h)),
        pl.BlockSpec((1, t, hd), lambda bi, h, s, idx, pt: (bi, 0, 2 * MB_HEADS + h)),
        cache_spec(), cache_spec(),
        pl.BlockSpec((1, 1, 1, PAGE_SIZE), lambda bi, h, s, idx, pt: (s // per_q, h, 0, page_of(bi, h, s, idx, pt))),
        pl.BlockSpec((1, 1, 1, PAGE_SIZE), lambda bi, h, s, idx, pt: (s // per_q, h, 0, n_pages)),
    ]
    return pl.pallas_call(
        functools.partial(_sample_attn_kernel, past=past, nq=t),
        out_shape=jax.ShapeDtypeStruct((b, t, d), F32),
        grid_spec=pltpu.PrefetchScalarGridSpec(
            num_scalar_prefetch=2, grid=(b, MB_HEADS, t * per_q),
            in_specs=in_specs,
            out_specs=pl.BlockSpec((1, t, hd), lambda bi, h, s, idx, pt: (bi, 0, h)),
            scratch_shapes=[pltpu.VMEM((1, 1), F32), pltpu.VMEM((1, 1), F32), pltpu.VMEM((1, hd), F32)]),
        compiler_params=_cparams("parallel", "parallel", "arbitrary"),
        name="sample_attention",
    )(idx_flat, pt_flat, qn, kn, qkv, cache_k4, cache_v4, tbl, tbl)


def moba_sample_layer(x, ln_g, w_qkv, q_gain, k_gain, w_out, cache_k4, cache_v4, layer, page_table, tbl):
    b, t, d = x.shape
    n_pages = page_table.shape[1]
    past = n_pages * PAGE_SIZE
    assert n_pages % 2 == 0 and t <= 8
    x2 = x.reshape(b * t, d)
    qkv = norm_matmul(x2, ln_g, w_qkv).reshape(b, t, -1)
    qn, kn, _ = qk_norm(qkv, q_gain, k_gain, rows=t)
    pt_flat = page_table.reshape(-1)
    kmean = page_block_means(cache_k4, layer, pt_flat, b, n_pages)
    idx = sample_select(qn, kmean, past)[:, :, :t, :MB_TOPK].reshape(-1)
    o = sample_attention(qn, kn, qkv, cache_k4, cache_v4, layer, idx, pt_flat, tbl, n_pages)
    y = matmul_residual(o.reshape(b * t, d).astype(BF16), w_out, x2).reshape(b, t, d)
    new_k = kn.reshape(b, t, MB_HEADS, MB_HEAD_DIM)
    new_v = qkv[:, :, 2 * d:].reshape(b, t, MB_HEADS, MB_HEAD_DIM)
    return y, new_k, new_v


def kernel(x_prompt, x_sample, state_delta, state_conv, cache_k, cache_v, page_table, rel_bias,
           ln_mix, ln_mlp, dn_w_in, dn_conv_w, dn_a_log, dn_dt_bias, dn_norm_w, dn_w_out,
           mb_w_qkv, mb_q_norm, mb_k_norm, mb_w_out, mlp_w_up, mlp_w_down):
    depth = ln_mix.shape[0]
    bp, sp, _ = x_prompt.shape
    bd, td, _ = x_sample.shape
    n_pages = page_table.shape[1]
    past = n_pages * PAGE_SIZE
    nl, pool = cache_k.shape[:2]
    cache_k4 = cache_k.reshape(nl, pool, PAGE_SIZE, MB_HEADS * MB_HEAD_DIM)
    cache_v4 = cache_v.reshape(nl, pool, PAGE_SIZE, MB_HEADS * MB_HEAD_DIM)

    tbl_p = rel_bias_table(rel_bias, r0=sp - 1, nrow=1, ltab=sp + MB_BLOCK)
    tbl_p = tbl_p.reshape(MB_HEADS, 1, sp + MB_BLOCK)
    tbl_s = rel_bias_table(rel_bias, r0=past, nrow=td, ltab=past + PAGE_SIZE)
    tbl_s = tbl_s.reshape(td, MB_HEADS, 1, past + PAGE_SIZE)

    n_gate = 2 * DN_V_HEADS
    xp, xd = x_prompt, x_sample
    sdp, scp, sds, scs, kps, vps, kds, vds = [], [], [], [], [], [], [], []
    for i in range(depth):
        j = i // 2
        if i % 2 == 0:
            w_in = dn_w_in[j]
            w_main = w_in[:, :DN_CONV_DIM + DN_VAL_DIM].astype(BF16)
            w_gate = jnp.pad(w_in[:, DN_CONV_DIM + DN_VAL_DIM:], ((0, 0), (0, 128 - n_gate))).astype(BF16)
            w_out = dn_w_out[j].astype(BF16)
            args = (w_main, w_gate, dn_conv_w[j], dn_a_log[j], dn_dt_bias[j], dn_norm_w[j], w_out)
            buf0 = jnp.zeros((bp, CONV_W - 1, DN_CONV_DIM), F32)
            s0 = jnp.zeros((bp, DN_V_HEADS, DN_HEAD, DN_HEAD), F32)
            xp, buf_p, s_p = deltanet_layer(xp, ln_mix[i], *args, buf0, s0)
            xd, buf_d, s_d = deltanet_layer(xd, ln_mix[i], *args, state_conv[j], state_delta[j])
            sdp.append(s_p)
            scp.append(buf_p)
            sds.append(s_d)
            scs.append(buf_d)
        else:
            w_qkv = mb_w_qkv[j].astype(BF16)
            w_out = mb_w_out[j].astype(BF16)
            xp, kp, vp = moba_prompt_layer(xp, ln_mix[i], w_qkv, mb_q_norm[j], mb_k_norm[j], w_out, tbl_p)
            xd, kd, vd = moba_sample_layer(xd, ln_mix[i], w_qkv, mb_q_norm[j], mb_k_norm[j], w_out,
                                           cache_k4, cache_v4, j, page_table, tbl_s)
            kps.append(kp)
            vps.append(vp)
            kds.append(kd)
            vds.append(vd)
        w_up = mlp_w_up[i].astype(BF16)
        w_down = mlp_w_down[i].astype(BF16)
        xp = mlp_layer(xp, ln_mlp[i], w_up, w_down)
        xd = mlp_layer(xd, ln_mlp[i], w_up, w_down)
    return (xp, xd, jnp.stack(sdp), jnp.stack(scp), jnp.stack(kps), jnp.stack(vps),
            jnp.stack(sds), jnp.stack(scs), jnp.stack(kds), jnp.stack(vds))
```

```python
import functools
import math

import jax
import jax.numpy as jnp
from jax import lax
from jax.experimental import pallas as pl
from jax.experimental.pallas import tpu as pltpu

F32 = jnp.float32
BF16 = jnp.bfloat16

D_MODEL = 2048
DN_QK_HEADS = 16
DN_V_HEADS = 32
DN_HEAD = 128
DN_KEY_DIM = DN_QK_HEADS * DN_HEAD
DN_VAL_DIM = DN_V_HEADS * DN_HEAD
DN_CONV_DIM = 2 * DN_KEY_DIM + DN_VAL_DIM
CONV_W = 4
DN_CHUNK = 64
DN_PAIRS_PER_STEP = 4
MB_HEADS = 16
MB_HEAD_DIM = 128
MB_BLOCK = 256
MB_TOPK = 3
MB_HEADS_PER_STEP = 2
PAGE_SIZE = 128
REL_BUCKETS = 32
REL_MAX_DIST = 4096
EPS = 1e-6
NEG_INF = -1e30

VMEM_LIMIT_BYTES = 56 * 1024 * 1024


def _cparams(*sem):
    return pltpu.CompilerParams(dimension_semantics=sem, vmem_limit_bytes=VMEM_LIMIT_BYTES)


def _pick_tile(n, pref):
    if n <= pref:
        return n
    t = pref
    while n % t:
        t //= 2
    return t


def _norm_matmul_kernel(x_ref, g_ref, w_ref, o_ref, hn_ref, *, act):
    @pl.when(pl.program_id(1) == 0)
    def _():
        x = x_ref[...]
        ms = jnp.mean(x * x, axis=-1, keepdims=True)
        hn_ref[...] = (x * lax.rsqrt(ms + EPS) * g_ref[...]).astype(hn_ref.dtype)

    y = jnp.dot(hn_ref[...], w_ref[...], preferred_element_type=F32)
    if act:
        y = jnp.square(jnp.maximum(y, 0.0))
    o_ref[...] = y.astype(o_ref.dtype)


def norm_matmul(x, g, w, *, act=False, out_dtype=F32, tm=1024, tn=1024):
    m, d = x.shape
    n = w.shape[1]
    tm = _pick_tile(m, tm)
    tn = _pick_tile(n, tn)
    return pl.pallas_call(
        functools.partial(_norm_matmul_kernel, act=act),
        out_shape=jax.ShapeDtypeStruct((m, n), out_dtype),
        grid=(m // tm, n // tn),
        in_specs=[pl.BlockSpec((tm, d), lambda i, j: (i, 0)),
                  pl.BlockSpec((1, d), lambda i, j: (0, 0)),
                  pl.BlockSpec((d, tn), lambda i, j: (0, j))],
        out_specs=pl.BlockSpec((tm, tn), lambda i, j: (i, j)),
        scratch_shapes=[pltpu.VMEM((tm, d), BF16)],
        compiler_params=_cparams("parallel", "arbitrary"),
        name="norm_matmul",
    )(x, g.reshape(1, d), w)


def _matmul_res_kernel(a_ref, w_ref, r_ref, o_ref, acc_ref, *, nk):
    k = pl.program_id(2)

    @pl.when(k == 0)
    def _():
        acc_ref[...] = jnp.zeros_like(acc_ref)

    acc_ref[...] += jnp.dot(a_ref[...], w_ref[...], preferred_element_type=F32)

    @pl.when(k == nk - 1)
    def _():
        o_ref[...] = r_ref[...] + acc_ref[...]


def matmul_residual(a, w, res, *, tm=1024, tn=1024, tk=2048):
    m, kdim = a.shape
    n = w.shape[1]
    tm = _pick_tile(m, tm)
    tn = _pick_tile(n, tn)
    tk = _pick_tile(kdim, tk)
    nk = kdim // tk
    return pl.pallas_call(
        functools.partial(_matmul_res_kernel, nk=nk),
        out_shape=jax.ShapeDtypeStruct((m, n), F32),
        grid=(m // tm, n // tn, nk),
        in_specs=[pl.BlockSpec((tm, tk), lambda i, j, k: (i, k)),
                  pl.BlockSpec((tk, tn), lambda i, j, k: (k, j)),
                  pl.BlockSpec((tm, tn), lambda i, j, k: (i, j))],
        out_specs=pl.BlockSpec((tm, tn), lambda i, j, k: (i, j)),
        scratch_shapes=[pltpu.VMEM((tm, tn), F32)],
        compiler_params=_cparams("parallel", "parallel", "arbitrary"),
        name="matmul_residual",
    )(a, w, res)


def _softplus(x):
    return jnp.maximum(x, 0.0) + jnp.log1p(jnp.exp(-jnp.abs(x)))


def _silu(x):
    return x * jax.nn.sigmoid(x)


def _bdot(a, b):
    return jnp.dot(a.astype(BF16), b.astype(BF16), preferred_element_type=F32)


def _bdot_nt(a, b):
    return lax.dot_general(a.astype(BF16), b.astype(BF16), (((1,), (1,)), ((), ())),
                           preferred_element_type=F32)


def _bdot_tn(a, b):
    return lax.dot_general(a.astype(BF16), b.astype(BF16), (((0,), (0,)), ((), ())),
                           preferred_element_type=F32)


def _unit_lower_inverses(lmats):
    c = lmats[0].shape[0]
    row = lax.broadcasted_iota(jnp.int32, (c, c), 0)
    col = lax.broadcasted_iota(jnp.int32, (c, c), 1)

    def same_block(size):
        shift = size.bit_length() - 1
        return (row >> shift) == (col >> shift)

    eye = jnp.where(row == col, 1.0, 0.0)
    n1 = [jnp.where(same_block(8), l, 0.0) for l in lmats]
    n2 = [_bdot(a, a) for a in n1]
    t = [eye - a for a in n1]
    t = [ti + _bdot(ti, b) for ti, b in zip(t, n2)]
    n4 = [_bdot(b, b) for b in n2]
    t = [ti + _bdot(ti, b) for ti, b in zip(t, n4)]
    size = 16
    while size <= c:
        cross = same_block(size) & jnp.logical_not(same_block(size // 2))
        tc = [_bdot(ti, jnp.where(cross, l, 0.0)) for ti, l in zip(t, lmats)]
        t = [ti - _bdot(tci, ti) for ti, tci in zip(t, tc)]
        size *= 2
    return t


def _dn_core_kernel(alog_ref, dtb_ref,
                    q_ref, k_ref, v_ref, z_ref, gate_ref,
                    cwq_ref, cwk_ref, cwv_ref, nw_ref,
                    cbq_ref, cbk_ref, cbv_ref, s0_ref,
                    o_ref, sout_ref,
                    xq_ref, xk_ref, xv_ref, s_ref,
                    *, tb, t_valid, nt, npair):
    hg = pl.program_id(1)
    t = pl.program_id(2)
    c = DN_CHUNK
    nchunk = tb // c
    hd = DN_HEAD
    pad = 8
    tail = CONV_W - 1

    @pl.when(t == 0)
    def _():
        xq_ref[pad - tail:pad, :] = cbq_ref[0]
        xk_ref[pad - tail:pad, :] = cbk_ref[0]
        xv_ref[pad - tail:pad, :] = cbv_ref[0]
        s_ref[...] = s0_ref[0]

    xq_ref[pad:pad + tb, :] = q_ref[0]
    xk_ref[pad:pad + tb, :] = k_ref[0]
    xv_ref[pad:pad + tb, :] = v_ref[0]

    def conv(x_ref, cw_ref):
        acc = x_ref[pad - tail:pad - tail + tb, :] * cw_ref[0:1, :]
        for tap in range(1, CONV_W):
            acc = acc + x_ref[pad - tail + tap:pad - tail + tap + tb, :] * cw_ref[tap:tap + 1, :]
        return _silu(acc)

    q_all = conv(xq_ref, cwq_ref)
    k_all = conv(xk_ref, cwk_ref)
    v_all = conv(xv_ref, cwv_ref)

    xq_ref[pad - tail:pad, :] = xq_ref[pad + tb - tail:pad + tb, :]
    xk_ref[pad - tail:pad, :] = xk_ref[pad + tb - tail:pad + tb, :]
    xv_ref[pad - tail:pad, :] = xv_ref[pad + tb - tail:pad + tb, :]

    qs, ks = [], []
    for p in range(npair):
        qp = q_all[:, p * hd:(p + 1) * hd]
        kp = k_all[:, p * hd:(p + 1) * hd]
        qs.append(qp * lax.rsqrt(jnp.sum(qp * qp, axis=-1, keepdims=True) + EPS) * (hd ** -0.5))
        ks.append(kp * lax.rsqrt(jnp.sum(kp * kp, axis=-1, keepdims=True) + EPS))

    row = lax.broadcasted_iota(jnp.int32, (c, c), 0)
    col = lax.broadcasted_iota(jnp.int32, (c, c), 1)
    eye = row == col
    tril = row >= col
    strict = row > col
    lane_t = lax.broadcasted_iota(jnp.int32, (1, c), 1)

    qk_idx = [(ci, p) for ci in range(nchunk) for p in range(npair)]
    qcs = {(ci, p): qs[p][ci * c:(ci + 1) * c] for ci, p in qk_idx}
    kcs = {(ci, p): ks[p][ci * c:(ci + 1) * c] for ci, p in qk_idx}
    qks = {key: _bdot_nt(jnp.concatenate([qcs[key], kcs[key]], axis=0), kcs[key]) for key in qk_idx}
    systems = [(ci, p, r) for ci in range(nchunk) for p in range(npair) for r in range(2)]
    lmats, a_intras, beta_cols, egcs, kdecs, sdecs = [], [], [], [], [], []
    for ci, p, r in systems:
        head = 2 * (hg * npair + p) + r
        gates = gate_ref[0, p, ci]
        valid = (t * tb + ci * c + lane_t) < t_valid
        beta_row = jnp.where(valid, jax.nn.sigmoid(gates[r:r + 1, :]), 0.0)
        a_scale = -jnp.exp(jnp.full((1, c), alog_ref[head], F32))
        g_row = jnp.where(valid, a_scale * _softplus(gates[2 + r:3 + r, :] + dtb_ref[head]), 0.0)
        gc_col = jnp.sum(jnp.where(tril, g_row, 0.0), axis=1, keepdims=True)
        gc_row = jnp.sum(jnp.where(eye, gc_col, 0.0), axis=0, keepdims=True)
        beta_col = jnp.sum(jnp.where(eye, beta_row, 0.0), axis=1, keepdims=True)
        decay = jnp.exp(jnp.where(tril, gc_col - gc_row, NEG_INF))
        qkt, kkt = qks[ci, p][:c], qks[ci, p][c:]
        lmats.append(jnp.where(strict, kkt * beta_col * decay, 0.0))
        a_intras.append(jnp.where(tril, qkt * decay, 0.0))
        g_last = gc_col[c - 1:c, :]
        beta_cols.append(beta_col)
        egcs.append(jnp.exp(gc_col))
        kdecs.append(kcs[ci, p] * jnp.exp(g_last - gc_col))
        sdecs.append(jnp.exp(g_last))
    tinvs = _unit_lower_inverses(lmats)
    uws = []
    for i, (ci, p, r) in enumerate(systems):
        vc = v_all[ci * c:(ci + 1) * c, (2 * p + r) * hd:(2 * p + r + 1) * hd]
        rhs = jnp.concatenate([vc * beta_cols[i], kcs[ci, p] * (beta_cols[i] * egcs[i])], axis=1)
        uws.append(_bdot(tinvs[i], rhs))

    nh = 2 * npair
    states = [s_ref[h] for h in range(nh)]
    lhs = [jnp.concatenate([uws[i][:, hd:], qcs[ci, p] * egcs[i]], axis=0)
           for i, (ci, p, r) in enumerate(systems)]
    kdts = [kd.T for kd in kdecs]
    outs = []
    for ci in range(nchunk):
        ids = range(ci * nh, (ci + 1) * nh)
        ws_qs = [_bdot(lhs[i], states[i - ci * nh]) for i in ids]
        v_new = [uws[i][:, :hd] - x[:c] for i, x in zip(ids, ws_qs)]
        outs += [x[c:] + _bdot(a_intras[i], vn) for i, x, vn in zip(ids, ws_qs, v_new)]
        states = [states[i - ci * nh] * sdecs[i] + _bdot(kdts[i], vn) for i, vn in zip(ids, v_new)]
    for h in range(nh):
        s_ref[h] = states[h]

    for i, (ci, p, r) in enumerate(systems):
        o = outs[i]
        cols = slice((2 * p + r) * hd, (2 * p + r + 1) * hd)
        zc = z_ref[0, ci * c:(ci + 1) * c, cols]
        o = o * lax.rsqrt(jnp.mean(o * o, axis=-1, keepdims=True) + EPS) * nw_ref[...] * _silu(zc)
        o_ref[0, ci * c:(ci + 1) * c, cols] = o.astype(o_ref.dtype)

    @pl.when(t == nt - 1)
    def _():
        sout_ref[0] = s_ref[...]


def dn_core(proj, gates_t, conv_w, a_log, dt_bias, norm_w, conv_buf, s0, *, t_valid, tb, npair):
    b, t, _ = proj.shape
    nt = t // tb
    ng = DN_QK_HEADS // npair
    wq = DN_HEAD * npair
    wv = 2 * wq
    kern = functools.partial(_dn_core_kernel, tb=tb, t_valid=t_valid, nt=nt, npair=npair)
    smem = pl.BlockSpec(memory_space=pltpu.SMEM)
    k_blk0 = DN_KEY_DIM // wq
    v_blk0 = 2 * DN_KEY_DIM // wv
    z_blk0 = DN_CONV_DIM // wv
    in_specs = [
        smem, smem,
        pl.BlockSpec((1, tb, wq), lambda bi, h, ti: (bi, ti, h)),
        pl.BlockSpec((1, tb, wq), lambda bi, h, ti: (bi, ti, k_blk0 + h)),
        pl.BlockSpec((1, tb, wv), lambda bi, h, ti: (bi, ti, v_blk0 + h)),
        pl.BlockSpec((1, tb, wv), lambda bi, h, ti: (bi, ti, z_blk0 + h)),
        pl.BlockSpec((1, npair, tb // DN_CHUNK, 4, DN_CHUNK), lambda bi, h, ti: (bi, h, ti, 0, 0)),
        pl.BlockSpec((CONV_W, wq), lambda bi, h, ti: (0, h)),
        pl.BlockSpec((CONV_W, wq), lambda bi, h, ti: (0, k_blk0 + h)),
        pl.BlockSpec((CONV_W, wv), lambda bi, h, ti: (0, v_blk0 + h)),
        pl.BlockSpec((1, DN_HEAD), lambda bi, h, ti: (0, 0)),
        pl.BlockSpec((1, CONV_W - 1, wq), lambda bi, h, ti: (bi, 0, h)),
        pl.BlockSpec((1, CONV_W - 1, wq), lambda bi, h, ti: (bi, 0, k_blk0 + h)),
        pl.BlockSpec((1, CONV_W - 1, wv), lambda bi, h, ti: (bi, 0, v_blk0 + h)),
        pl.BlockSpec((1, 2 * npair, DN_HEAD, DN_HEAD), lambda bi, h, ti: (bi, h, 0, 0)),
    ]
    out_specs = [
        pl.BlockSpec((1, tb, wv), lambda bi, h, ti: (bi, ti, h)),
        pl.BlockSpec((1, 2 * npair, DN_HEAD, DN_HEAD), lambda bi, h, ti: (bi, h, 0, 0)),
    ]
    return pl.pallas_call(
        kern,
        out_shape=[jax.ShapeDtypeStruct((b, t, DN_VAL_DIM), BF16),
                   jax.ShapeDtypeStruct((b, DN_V_HEADS, DN_HEAD, DN_HEAD), F32)],
        grid=(b, ng, nt),
        in_specs=in_specs,
        out_specs=out_specs,
        scratch_shapes=[pltpu.VMEM((tb + 8, wq), F32), pltpu.VMEM((tb + 8, wq), F32),
                        pltpu.VMEM((tb + 8, wv), F32), pltpu.VMEM((2 * npair, DN_HEAD, DN_HEAD), F32)],
        compiler_params=_cparams("parallel", "parallel", "arbitrary"),
        name="dn_core",
    )(a_log, dt_bias, proj, proj, proj, proj, gates_t,
      conv_w, conv_w, conv_w, norm_w.reshape(1, DN_HEAD),
      conv_buf, conv_buf, conv_buf, s0)


def deltanet_layer(x, ln_g, w_main, w_gate, conv_w, a_log, dt_bias, norm_w, w_out, conv_buf, s0):
    b, t, d = x.shape
    x2 = x.reshape(b * t, d)
    proj = norm_matmul(x2, ln_g, w_main).reshape(b, t, -1)
    gates = norm_matmul(x2, ln_g, w_gate).reshape(b, t, -1)[:, :, :2 * DN_V_HEADS]
    new_buf = jnp.concatenate([conv_buf, proj[:, :, :DN_CONV_DIM]], axis=1)[:, -(CONV_W - 1):]
    tp = -(-t // DN_CHUNK) * DN_CHUNK
    if tp != t:
        proj = jnp.pad(proj, ((0, 0), (0, tp - t), (0, 0)))
        gates = jnp.pad(gates, ((0, 0), (0, tp - t), (0, 0)))
    g5 = gates.reshape(b, tp // DN_CHUNK, DN_CHUNK, 2, DN_QK_HEADS, 2)
    gates_t = jnp.transpose(g5, (0, 4, 1, 3, 5, 2)).reshape(b, DN_QK_HEADS, tp // DN_CHUNK, 4, DN_CHUNK)
    tb = _pick_tile(tp, 256)
    o, s_new = dn_core(proj, gates_t, conv_w, a_log, dt_bias, norm_w, conv_buf, s0,
                       t_valid=t, tb=tb, npair=DN_PAIRS_PER_STEP)
    o = o[:, :t].reshape(b * t, DN_VAL_DIM)
    y = matmul_residual(o, w_out, x2)
    return y.reshape(b, t, d), new_buf, s_new


def mlp_layer(x, ln_g, w_up, w_down):
    b, t, d = x.shape
    x2 = x.reshape(b * t, d)
    hid = norm_matmul(x2, ln_g, w_up, act=True, out_dtype=BF16)
    return matmul_residual(hid, w_down, x2).reshape(b, t, d)


def _rel_bias_table_kernel(rbt_ref, o_ref, *, r0, ltab):
    rel = (r0 + pl.program_id(0)) - lax.broadcasted_iota(jnp.int32, (1, ltab), 1)
    n = jnp.maximum(rel, 0)
    max_exact = REL_BUCKETS // 2
    nf = jnp.maximum(n, max_exact).astype(F32)
    large = max_exact + (jnp.log(nf / max_exact) / math.log(REL_MAX_DIST / max_exact)
                         * (REL_BUCKETS - max_exact)).astype(jnp.int32)
    large = jnp.minimum(large, REL_BUCKETS - 1)
    bucket = jnp.where(n < max_exact, n, large)
    rbt = rbt_ref[...]
    out = jnp.zeros((MB_HEADS, ltab), F32)
    for bkt in range(REL_BUCKETS):
        out = jnp.where(bucket == bkt, rbt[:, bkt:bkt + 1], out)
    o_ref[0] = out


def rel_bias_table(rel_bias, *, r0, nrow, ltab):
    return pl.pallas_call(
        functools.partial(_rel_bias_table_kernel, r0=r0, ltab=ltab),
        out_shape=jax.ShapeDtypeStruct((nrow, MB_HEADS, ltab), F32),
        grid=(nrow,),
        in_specs=[pl.BlockSpec((MB_HEADS, REL_BUCKETS), lambda i: (0, 0))],
        out_specs=pl.BlockSpec((1, MB_HEADS, ltab), lambda i: (i, 0, 0)),
        compiler_params=_cparams("arbitrary"),
        name="rel_bias_table",
    )(rel_bias.T)


def _qk_norm_kernel(q_ref, k_ref, v_ref, qg_ref, kg_ref, kn_ref, qa_ref, ka_ref, va_ref, km_ref):
    qg = qg_ref[...]
    kg = kg_ref[...]
    va_ref[...] = v_ref[...].astype(va_ref.dtype)
    for h in range(MB_HEADS):
        sl = slice(h * MB_HEAD_DIM, (h + 1) * MB_HEAD_DIM)
        qh = q_ref[0, :, sl]
        kh = k_ref[0, :, sl]
        qn = qh * lax.rsqrt(jnp.mean(qh * qh, axis=-1, keepdims=True) + EPS) * qg
        kn = kh * lax.rsqrt(jnp.mean(kh * kh, axis=-1, keepdims=True) + EPS) * kg
        kn_ref[0, :, sl] = kn
        qa_ref[0, :, sl] = qn.astype(qa_ref.dtype)
        ka_ref[0, :, sl] = kn.astype(ka_ref.dtype)
        km_ref[0, 0, :, sl] = jnp.sum(kn, axis=0, keepdims=True) / MB_BLOCK


def qk_norm(qkv, q_gain, k_gain, *, rows, attn_dtype):
    b, t, _ = qkv.shape
    nb = t // rows
    d = MB_HEADS * MB_HEAD_DIM
    row_spec = [pl.BlockSpec((1, rows, d), functools.partial(lambda bi, i, c: (bi, i, c), c=c))
                for c in range(3)]
    gain_spec = pl.BlockSpec((1, MB_HEAD_DIM), lambda bi, i: (0, 0))
    return pl.pallas_call(
        _qk_norm_kernel,
        out_shape=[jax.ShapeDtypeStruct((b, t, d), F32)]
        + [jax.ShapeDtypeStruct((b, t, d), attn_dtype)] * 3
        + [jax.ShapeDtypeStruct((b, nb, 1, d), F32)],
        grid=(b, nb),
        in_specs=row_spec + [gain_spec, gain_spec],
        out_specs=[row_spec[0]] * 4 + [pl.BlockSpec((1, 1, 1, d), lambda bi, i: (bi, i, 0, 0))],
        compiler_params=_cparams("parallel", "parallel"),
        name="qk_norm",
    )(qkv, qkv, qkv, q_gain.reshape(1, -1), k_gain.reshape(1, -1))


def _block_scores(q, km):
    nb = km.shape[0]
    km = jnp.concatenate([km, jnp.zeros((128 - nb, km.shape[1]), F32)], axis=0)
    return _bdot_nt(q, km)


def _block_ranks(scores, ncand, nb):
    lane = lax.broadcasted_iota(jnp.int32, scores.shape, 1)
    sm = jnp.where(lane < ncand, scores, NEG_INF)
    cnt = jnp.zeros(scores.shape, jnp.int32)
    for m in range(nb):
        col = sm[:, m:m + 1]
        better = (col > sm) | ((col == sm) & (lane > m))
        cnt = cnt + jnp.where(better & (m < ncand), 1, 0)
    return cnt


def _moba_prompt_kernel(tbl_ref, q_ref, k_ref, v_ref, km_ref, o_ref, bias_ref, wide_ref, *, nblk, nhead):
    b = pl.program_id(1)
    qi = pl.program_id(2)
    blk = MB_BLOCK
    hd = MB_HEAD_DIM
    scale = hd ** -0.5
    heads = range(nhead)

    @pl.when(b == 0)
    def _():
        start = pl.multiple_of((nblk - 1 - qi) * blk, blk)
        for h in heads:
            u = tbl_ref[h, :, pl.ds(start, 2 * blk)]
            wide_ref[...] = jnp.broadcast_to(u, (blk, 2 * blk))
            tile = pltpu.roll(wide_ref[...], blk + 1, 1, stride=1, stride_axis=0)
            bias_ref[h, qi] = tile[:, :blk]

    def cols(h):
        return slice(h * hd, (h + 1) * hd)

    qs = [q_ref[0, :, cols(h)] for h in heads]
    blk_id = lax.broadcasted_iota(jnp.int32, (nblk, blk), 0)
    pens = []
    for h in heads:
        sm = jnp.where(blk_id < qi, _bdot_nt(km_ref[0, :, 0, cols(h)], qs[h]), NEG_INF)
        cnt = jnp.zeros((nblk, blk), jnp.int32)
        for m in range(nblk):
            rowm = sm[m:m + 1, :]
            better = (rowm > sm) | ((rowm == sm) & (blk_id > m))
            cnt = cnt + jnp.where(better & (m < qi), 1, 0)
        pen_t = jnp.where((blk_id < qi) & (cnt < MB_TOPK), 0.0, NEG_INF)
        pens.append(jnp.concatenate([pen_t, jnp.full((128 - nblk, blk), NEG_INF, F32)], axis=0).T.astype(BF16))

    row = lax.broadcasted_iota(jnp.int32, (blk, blk), 0)
    col = lax.broadcasted_iota(jnp.int32, (blk, blk), 1)
    own0 = pl.multiple_of(qi * blk, blk)
    ss = [_bdot_nt(qs[h], k_ref[0, pl.ds(own0, blk), cols(h)]) for h in heads]
    ss = [jnp.where(row >= col, ss[h] * scale + bias_ref[h, 0], NEG_INF) for h in heads]
    m0 = [jnp.max(x, axis=1, keepdims=True) for x in ss]
    ps = [jnp.exp(x - mm) for x, mm in zip(ss, m0)]
    l0 = [jnp.sum(p, axis=1, keepdims=True) for p in ps]
    acc0 = [_bdot(ps[h], v_ref[0, pl.ds(own0, blk), cols(h)]) for h in heads]

    erow = lax.broadcasted_iota(jnp.int32, (128, 2 * blk), 0)
    ehalf = lax.broadcasted_iota(jnp.int32, (128, 2 * blk), 1) // blk

    def body(j, carry):
        ms, ls, accs = carry
        n0 = 2 * j
        r0 = pl.multiple_of(n0 * blk, 2 * blk)
        d1 = jnp.maximum(qi - n0 - 1, 0)
        expand = jnp.where(erow == n0 + ehalf, 1.0, 0.0).astype(BF16)
        pen2 = [jnp.dot(pens[h], expand, preferred_element_type=F32) for h in heads]
        ss = [_bdot_nt(qs[h], k_ref[0, pl.ds(r0, 2 * blk), cols(h)]) for h in heads]
        ss = [ss[h] * scale + (pen2[h] + jnp.concatenate([bias_ref[h, qi - n0], bias_ref[h, d1]], axis=1))
              for h in heads]
        m_new = [jnp.maximum(ms[h], jnp.max(ss[h], axis=1, keepdims=True)) for h in heads]
        alpha = [jnp.exp(ms[h] - m_new[h]) for h in heads]
        ps = [jnp.exp(ss[h] - m_new[h]) for h in heads]
        ls = tuple(ls[h] * alpha[h] + jnp.sum(ps[h], axis=1, keepdims=True) for h in heads)
        pv = [_bdot(ps[h], v_ref[0, pl.ds(r0, 2 * blk), cols(h)]) for h in heads]
        accs = tuple(accs[h] * alpha[h] + pv[h] for h in heads)
        return tuple(m_new), ls, accs

    ms, ls, accs = lax.fori_loop(0, (qi + 1) // 2, body, (tuple(m0), tuple(l0), tuple(acc0)))
    for h in heads:
        o_ref[0, :, cols(h)] = (accs[h] / ls[h]).astype(o_ref.dtype)


def moba_prompt_attention(q, k, v, kmean, tbl):
    b, s, d = q.shape
    nblk = s // MB_BLOCK
    assert nblk % 2 == 0 and nblk <= 128
    nh = MB_HEADS_PER_STEP
    w = nh * MB_HEAD_DIM
    return pl.pallas_call(
        functools.partial(_moba_prompt_kernel, nblk=nblk, nhead=nh),
        out_shape=jax.ShapeDtypeStruct((b, s, d), BF16),
        grid=(MB_HEADS // nh, b, nblk),
        in_specs=[pl.BlockSpec((nh, 1, s + MB_BLOCK), lambda h, bi, i: (h, 0, 0)),
                  pl.BlockSpec((1, MB_BLOCK, w), lambda h, bi, i: (bi, i, h)),
                  pl.BlockSpec((1, s, w), lambda h, bi, i: (bi, 0, h)),
                  pl.BlockSpec((1, s, w), lambda h, bi, i: (bi, 0, h)),
                  pl.BlockSpec((1, nblk, 1, w), lambda h, bi, i: (bi, 0, 0, h))],
        out_specs=pl.BlockSpec((1, MB_BLOCK, w), lambda h, bi, i: (bi, i, h)),
        scratch_shapes=[pltpu.VMEM((nh, nblk, MB_BLOCK, MB_BLOCK), F32),
                        pltpu.VMEM((MB_BLOCK, 2 * MB_BLOCK), F32)],
        compiler_params=_cparams("parallel", "arbitrary", "arbitrary"),
        name="moba_prompt_attention",
    )(tbl, q, k, v, kmean)


def moba_prompt_layer(x, ln_g, w_qkv, q_gain, k_gain, w_out, tbl):
    b, s, d = x.shape
    assert s % MB_BLOCK == 0
    x2 = x.reshape(b * s, d)
    qkv = norm_matmul(x2, ln_g, w_qkv).reshape(b, s, -1)
    kn, qa, ka, va, kmean = qk_norm(qkv, q_gain, k_gain, rows=MB_BLOCK, attn_dtype=BF16)
    o = moba_prompt_attention(qa, ka, va, kmean, tbl)
    y = matmul_residual(o.reshape(b * s, d), w_out, x2).reshape(b, s, d)
    new_k = kn.reshape(b, s, MB_HEADS, MB_HEAD_DIM)
    new_v = qkv[:, :, 2 * d:].reshape(b, s, MB_HEADS, MB_HEAD_DIM)
    return y, new_k, new_v


def _page_block_mean_kernel(pt_ref, a_ref, b_ref, o_ref):
    del pt_ref
    o_ref[0, 0] = (jnp.sum(a_ref[0, 0], axis=0) + jnp.sum(b_ref[0, 0], axis=0)) / MB_BLOCK


def page_block_means(cache_k, layer, pt_flat, bsz, n_pages):
    nb = n_pages // 2
    page_shape = cache_k.shape[2:]

    def page_spec(off):
        return pl.BlockSpec((1, 1) + page_shape,
                            lambda bi, n, pt: (layer, pt[bi * n_pages + 2 * n + off], 0, 0, 0))

    return pl.pallas_call(
        _page_block_mean_kernel,
        out_shape=jax.ShapeDtypeStruct((bsz, nb) + page_shape[1:], F32),
        grid_spec=pltpu.PrefetchScalarGridSpec(
            num_scalar_prefetch=1, grid=(bsz, nb),
            in_specs=[page_spec(0), page_spec(1)],
            out_specs=pl.BlockSpec((1, 1) + page_shape[1:], lambda bi, n, pt: (bi, n, 0, 0))),
        compiler_params=_cparams("parallel", "parallel"),
        name="page_block_means",
    )(pt_flat, cache_k, cache_k)


def _sample_select_kernel(q_ref, km_ref, o_ref, *, past, nb):
    lane = lax.broadcasted_iota(jnp.int32, (8, 128), 1)
    ncand = (past + lax.broadcasted_iota(jnp.int32, (8, 1), 0)) // MB_BLOCK
    cand = lane < ncand
    t = q_ref.shape[1]
    for h in range(MB_HEADS):
        sl = slice(h * MB_HEAD_DIM, (h + 1) * MB_HEAD_DIM)
        q = jnp.concatenate([q_ref[0, :, sl], jnp.zeros((8 - t, MB_HEAD_DIM), F32)], axis=0)
        rank = _block_ranks(_block_scores(q, km_ref[0, :, 0, sl]), ncand, nb)
        out = jnp.zeros((8, 128), jnp.int32)
        for r in range(MB_TOPK):
            idx = jnp.sum(jnp.where(cand & (rank == r), lane, 0), axis=1, keepdims=True)
            out = jnp.where(lane == r, idx, out)
        o_ref[0, h] = out


def sample_select(qn, kmean, past):
    b, t, d = qn.shape
    nb = kmean.shape[1]
    assert t <= 8 and nb <= 128
    return pl.pallas_call(
        functools.partial(_sample_select_kernel, past=past, nb=nb),
        out_shape=jax.ShapeDtypeStruct((b, MB_HEADS, 8, 128), jnp.int32),
        grid=(b,),
        in_specs=[pl.BlockSpec((1, t, d), lambda bi: (bi, 0, 0)),
                  pl.BlockSpec((1, nb, 1, d), lambda bi: (bi, 0, 0, 0))],
        out_specs=pl.BlockSpec((1, MB_HEADS, 8, 128), lambda bi: (bi, 0, 0, 0)),
        compiler_params=_cparams("parallel"),
        name="sample_select",
    )(qn, kmean)


def _sample_attn_kernel(idx_ref, pt_ref, q_ref, kn_ref, vn_ref, tbl_ref, ck_ref, cv_ref,
                        o_ref, kbuf, vbuf, sem, *, layer, n_pages, nq):
    g = pl.program_id(0)
    nsteps = pl.num_programs(0)
    slot = g % 2
    per_q = MB_TOPK * 2
    past = n_pages * PAGE_SIZE
    scale = MB_HEAD_DIM ** -0.5

    def logical_page(gg, t, r, pg):
        return idx_ref[(gg * nq + t) * MB_TOPK + r] * 2 + pg

    def page_copies(gg, sl):
        b = gg // MB_HEADS
        h = gg % MB_HEADS
        out = []
        for t in range(nq):
            for r in range(MB_TOPK):
                for pg in range(2):
                    i = (t * MB_TOPK + r) * 2 + pg
                    phys = pt_ref[b * n_pages + logical_page(gg, t, r, pg)]
                    out.append(pltpu.make_async_copy(ck_ref.at[layer, phys, :, h, :], kbuf.at[sl, i], sem.at[sl, 0]))
                    out.append(pltpu.make_async_copy(cv_ref.at[layer, phys, :, h, :], vbuf.at[sl, i], sem.at[sl, 1]))
        return out

    @pl.when(g == 0)
    def _():
        for cp in page_copies(g, slot):
            cp.start()

    @pl.when(g + 1 < nsteps)
    def _():
        for cp in page_copies(g + 1, 1 - slot):
            cp.start()

    for cp in page_copies(g, slot):
        cp.wait()

    zpad = jnp.zeros((PAGE_SIZE - nq, MB_HEAD_DIM), F32)
    kown = jnp.concatenate([kn_ref[0], zpad], axis=0)
    vown = jnp.concatenate([vn_ref[0], zpad], axis=0)
    lane = lax.broadcasted_iota(jnp.int32, (1, PAGE_SIZE), 1)
    own_bias0 = pl.multiple_of(n_pages * PAGE_SIZE, PAGE_SIZE)
    for t in range(nq):
        q8 = jnp.broadcast_to(q_ref[0, t:t + 1, :], (8, MB_HEAD_DIM))
        kt = kbuf[slot, t * per_q:(t + 1) * per_q].reshape(per_q * PAGE_SIZE, MB_HEAD_DIM)
        vt = vbuf[slot, t * per_q:(t + 1) * per_q].reshape(per_q * PAGE_SIZE, MB_HEAD_DIM)
        bias, oks = [], []
        for r in range(MB_TOPK):
            ok = r < min((past + t) // MB_BLOCK, MB_TOPK)
            for pg in range(2):
                col0 = pl.multiple_of(logical_page(g, t, r, pg) * PAGE_SIZE, PAGE_SIZE)
                bias.append(tbl_ref[t, 0, :, pl.ds(col0, PAGE_SIZE)])
                oks.append(jnp.full((1, PAGE_SIZE), ok))
        bias.append(tbl_ref[t, 0, :, pl.ds(own_bias0, PAGE_SIZE)])
        oks.append(lane <= t)
        bias = jnp.concatenate(bias, axis=1)
        okm = jnp.concatenate(oks, axis=1)
        sc = jnp.concatenate([_bdot_nt(q8, kt)[0:1], _bdot_nt(q8, kown)[0:1]], axis=1) * scale + bias
        sc = jnp.where(okm, sc, NEG_INF)
        p = jnp.exp(sc - jnp.max(sc, axis=1, keepdims=True))
        l = jnp.sum(p, axis=1, keepdims=True)
        p8 = jnp.broadcast_to(p, (8, p.shape[1]))
        acc = _bdot(p8[:, :per_q * PAGE_SIZE], vt)[0:1] + _bdot(p8[:, per_q * PAGE_SIZE:], vown)[0:1]
        o_ref[0, t:t + 1, :] = acc / l


def sample_attention(qn, kn, vn, cache_k, cache_v, layer, idx_flat, pt_flat, tbl, n_pages):
    b, t, d = qn.shape
    hd = MB_HEAD_DIM
    ngather = t * MB_TOPK * 2
    row_spec = pl.BlockSpec((1, t, hd), lambda g, idx, pt: (g // MB_HEADS, 0, g % MB_HEADS))
    in_specs = [
        row_spec, row_spec, row_spec,
        pl.BlockSpec((t, 1, 1, tbl.shape[-1]), lambda g, idx, pt: (0, g % MB_HEADS, 0, 0)),
        pl.BlockSpec(memory_space=pl.ANY), pl.BlockSpec(memory_space=pl.ANY),
    ]
    return pl.pallas_call(
        functools.partial(_sample_attn_kernel, layer=layer, n_pages=n_pages, nq=t),
        out_shape=jax.ShapeDtypeStruct((b, t, d), F32),
        grid_spec=pltpu.PrefetchScalarGridSpec(
            num_scalar_prefetch=2, grid=(b * MB_HEADS,),
            in_specs=in_specs,
            out_specs=row_spec,
            scratch_shapes=[pltpu.VMEM((2, ngather, PAGE_SIZE, hd), F32),
                            pltpu.VMEM((2, ngather, PAGE_SIZE, hd), F32),
                            pltpu.SemaphoreType.DMA((2, 2))]),
        compiler_params=_cparams("arbitrary"),
        name="sample_attention",
    )(idx_flat, pt_flat, qn, kn, vn, tbl, cache_k, cache_v)


def moba_sample_layer(x, ln_g, w_qkv, q_gain, k_gain, w_out, cache_k, cache_v, layer, page_table, tbl):
    b, t, d = x.shape
    n_pages = page_table.shape[1]
    past = n_pages * PAGE_SIZE
    assert n_pages % 2 == 0 and t <= 8
    x2 = x.reshape(b * t, d)
    qkv = norm_matmul(x2, ln_g, w_qkv).reshape(b, t, -1)
    kn, qn, _, vn, _ = qk_norm(qkv, q_gain, k_gain, rows=t, attn_dtype=F32)
    pt_flat = page_table.reshape(-1)
    kmean = page_block_means(cache_k, layer, pt_flat, b, n_pages).reshape(b, n_pages // 2, 1, d)
    idx = sample_select(qn, kmean, past)[:, :, :t, :MB_TOPK].reshape(-1)
    o = sample_attention(qn, kn, vn, cache_k, cache_v, layer, idx, pt_flat, tbl, n_pages)
    y = matmul_residual(o.reshape(b * t, d).astype(BF16), w_out, x2).reshape(b, t, d)
    new_k = kn.reshape(b, t, MB_HEADS, MB_HEAD_DIM)
    new_v = qkv[:, :, 2 * d:].reshape(b, t, MB_HEADS, MB_HEAD_DIM)
    return y, new_k, new_v


def kernel(x_prompt, x_sample, state_delta, state_conv, cache_k, cache_v, page_table, rel_bias,
           ln_mix, ln_mlp, dn_w_in, dn_conv_w, dn_a_log, dn_dt_bias, dn_norm_w, dn_w_out,
           mb_w_qkv, mb_q_norm, mb_k_norm, mb_w_out, mlp_w_up, mlp_w_down):
    depth = ln_mix.shape[0]
    bp, sp, _ = x_prompt.shape
    bd, td, _ = x_sample.shape
    n_pages = page_table.shape[1]
    past = n_pages * PAGE_SIZE

    tbl_p = rel_bias_table(rel_bias, r0=sp - 1, nrow=1, ltab=sp + MB_BLOCK)
    tbl_p = tbl_p.reshape(MB_HEADS, 1, sp + MB_BLOCK)
    tbl_s = rel_bias_table(rel_bias, r0=past, nrow=td, ltab=past + PAGE_SIZE)
    tbl_s = tbl_s.reshape(td, MB_HEADS, 1, past + PAGE_SIZE)

    n_gate = 2 * DN_V_HEADS
    xp, xd = x_prompt, x_sample
    sdp, scp, sds, scs, kps, vps, kds, vds = [], [], [], [], [], [], [], []
    for i in range(depth):
        j = i // 2
        if i % 2 == 0:
            w_in = dn_w_in[j]
            w_main = w_in[:, :DN_CONV_DIM + DN_VAL_DIM].astype(BF16)
            w_gate = jnp.pad(w_in[:, DN_CONV_DIM + DN_VAL_DIM:], ((0, 0), (0, 128 - n_gate))).astype(BF16)
            w_out = dn_w_out[j].astype(BF16)
            args = (w_main, w_gate, dn_conv_w[j], dn_a_log[j], dn_dt_bias[j], dn_norm_w[j], w_out)
            buf0 = jnp.zeros((bp, CONV_W - 1, DN_CONV_DIM), F32)
            s0 = jnp.zeros((bp, DN_V_HEADS, DN_HEAD, DN_HEAD), F32)
            xp, buf_p, s_p = deltanet_layer(xp, ln_mix[i], *args, buf0, s0)
            xd, buf_d, s_d = deltanet_layer(xd, ln_mix[i], *args, state_conv[j], state_delta[j])
            sdp.append(s_p)
            scp.append(buf_p)
            sds.append(s_d)
            scs.append(buf_d)
        else:
            w_qkv = mb_w_qkv[j].astype(BF16)
            w_out = mb_w_out[j].astype(BF16)
            xp, kp, vp = moba_prompt_layer(xp, ln_mix[i], w_qkv, mb_q_norm[j], mb_k_norm[j], w_out, tbl_p)
            xd, kd, vd = moba_sample_layer(xd, ln_mix[i], w_qkv, mb_q_norm[j], mb_k_norm[j], w_out,
                                           cache_k, cache_v, j, page_table, tbl_s)
            kps.append(kp)
            vps.append(vp)
            kds.append(kd)
            vds.append(vd)
        w_up = mlp_w_up[i].astype(BF16)
        w_down = mlp_w_down[i].astype(BF16)
        xp = mlp_layer(xp, ln_mlp[i], w_up, w_down)
        xd = mlp_layer(xd, ln_mlp[i], w_up, w_down)
    return (xp, xd, jnp.stack(sdp), jnp.stack(scp), jnp.stack(kps), jnp.stack(vps),
            jnp.stack(sds), jnp.stack(scs), jnp.stack(kds), jnp.stack(vds))
```

```python
import functools
import math

import jax
import jax.numpy as jnp
from jax import lax
from jax.experimental import pallas as pl
from jax.experimental.pallas import tpu as pltpu

F32 = jnp.float32
BF16 = jnp.bfloat16

D_MODEL = 2048
DN_QK_HEADS = 16
DN_V_HEADS = 32
DN_HEAD = 128
DN_KEY_DIM = DN_QK_HEADS * DN_HEAD
DN_VAL_DIM = DN_V_HEADS * DN_HEAD
DN_CONV_DIM = 2 * DN_KEY_DIM + DN_VAL_DIM
CONV_W = 4
DN_CHUNK = 64
DN_PAIRS_PER_STEP = 4
MB_HEADS = 16
MB_HEAD_DIM = 128
MB_BLOCK = 256
MB_TOPK = 3
MB_HEADS_PER_STEP = 4
PAGE_SIZE = 128
REL_BUCKETS = 32
REL_MAX_DIST = 4096
EPS = 1e-6
NEG_INF = -1e30

VMEM_LIMIT_BYTES = 56 * 1024 * 1024


def _cparams(*sem):
    return pltpu.CompilerParams(dimension_semantics=sem, vmem_limit_bytes=VMEM_LIMIT_BYTES)


def _pick_tile(n, pref):
    if n <= pref:
        return n
    t = pref
    while n % t:
        t //= 2
    return t


def _cast_kernel(w_ref, o_ref, *, valid_cols):
    w = w_ref[0]
    if valid_cols < w.shape[1]:
        lane = lax.broadcasted_iota(jnp.int32, w.shape, 1)
        w = jnp.where(lane < valid_cols, w, 0.0)
    o_ref[...] = w.astype(o_ref.dtype)


def cast_weight(w, layer, col0=0, ncols=None):
    _, k, n = w.shape
    ncols = n if ncols is None else ncols
    tk = _pick_tile(k, 512)
    tn = _pick_tile(ncols, 2048)
    assert col0 % tn == 0 and (col0 + ncols <= n or ncols == tn)
    return pl.pallas_call(
        functools.partial(_cast_kernel, valid_cols=min(tn, n - col0)),
        out_shape=jax.ShapeDtypeStruct((k, ncols), BF16),
        grid=(k // tk, ncols // tn),
        in_specs=[pl.BlockSpec((1, tk, tn), lambda i, j: (layer, i, col0 // tn + j))],
        out_specs=pl.BlockSpec((tk, tn), lambda i, j: (i, j)),
        compiler_params=_cparams("parallel", "parallel"),
        name="cast_weight",
    )(w)


def _norm_matmul_kernel(x_ref, g_ref, w_ref, o_ref, hn_ref, *, act):
    @pl.when(pl.program_id(1) == 0)
    def _():
        x = x_ref[...]
        ms = jnp.mean(x * x, axis=-1, keepdims=True)
        hn_ref[...] = (x * lax.rsqrt(ms + EPS) * g_ref[...]).astype(hn_ref.dtype)

    y = jnp.dot(hn_ref[...], w_ref[...], preferred_element_type=F32)
    if act:
        y = jnp.square(jnp.maximum(y, 0.0))
    o_ref[...] = y.astype(o_ref.dtype)


def norm_matmul(x, g, w, *, act=False, out_dtype=F32, tm=1024, tn=1024):
    m, d = x.shape
    n = w.shape[1]
    tm = _pick_tile(m, tm)
    tn = _pick_tile(n, tn)
    return pl.pallas_call(
        functools.partial(_norm_matmul_kernel, act=act),
        out_shape=jax.ShapeDtypeStruct((m, n), out_dtype),
        grid=(m // tm, n // tn),
        in_specs=[pl.BlockSpec((tm, d), lambda i, j: (i, 0)),
                  pl.BlockSpec((1, d), lambda i, j: (0, 0)),
                  pl.BlockSpec((d, tn), lambda i, j: (0, j))],
        out_specs=pl.BlockSpec((tm, tn), lambda i, j: (i, j)),
        scratch_shapes=[pltpu.VMEM((tm, d), BF16)],
        compiler_params=_cparams("parallel", "arbitrary"),
        name="norm_matmul",
    )(x, g.reshape(1, d), w)


def _matmul_res_kernel(a_ref, w_ref, r_ref, o_ref, acc_ref, *, nk):
    k = pl.program_id(2)

    @pl.when(k == 0)
    def _():
        acc_ref[...] = jnp.zeros_like(acc_ref)

    acc_ref[...] += jnp.dot(a_ref[...], w_ref[...], preferred_element_type=F32)

    @pl.when(k == nk - 1)
    def _():
        o_ref[...] = r_ref[...] + acc_ref[...]


def matmul_residual(a, w, res, *, tm=1024, tn=1024, tk=2048):
    m, kdim = a.shape
    n = w.shape[1]
    tm = _pick_tile(m, tm)
    tn = _pick_tile(n, tn)
    tk = _pick_tile(kdim, tk)
    nk = kdim // tk
    return pl.pallas_call(
        functools.partial(_matmul_res_kernel, nk=nk),
        out_shape=jax.ShapeDtypeStruct((m, n), F32),
        grid=(m // tm, n // tn, nk),
        in_specs=[pl.BlockSpec((tm, tk), lambda i, j, k: (i, k)),
                  pl.BlockSpec((tk, tn), lambda i, j, k: (k, j)),
                  pl.BlockSpec((tm, tn), lambda i, j, k: (i, j))],
        out_specs=pl.BlockSpec((tm, tn), lambda i, j, k: (i, j)),
        scratch_shapes=[pltpu.VMEM((tm, tn), F32)],
        compiler_params=_cparams("parallel", "parallel", "arbitrary"),
        name="matmul_residual",
    )(a, w, res)


def _softplus(x):
    return jnp.maximum(x, 0.0) + jnp.log1p(jnp.exp(-jnp.abs(x)))


def _silu(x):
    return x * jax.nn.sigmoid(x)


def _bdot(a, b):
    return jnp.dot(a.astype(BF16), b.astype(BF16), preferred_element_type=F32)


def _bdot_nt(a, b):
    return lax.dot_general(a.astype(BF16), b.astype(BF16), (((1,), (1,)), ((), ())),
                           preferred_element_type=F32)


def _bdot_tn(a, b):
    return lax.dot_general(a.astype(BF16), b.astype(BF16), (((0,), (0,)), ((), ())),
                           preferred_element_type=F32)


def _unit_lower_inverses(lmats):
    c = lmats[0].shape[0]
    row = lax.broadcasted_iota(jnp.int32, (c, c), 0)
    col = lax.broadcasted_iota(jnp.int32, (c, c), 1)

    def same_block(size):
        shift = size.bit_length() - 1
        return (row >> shift) == (col >> shift)

    eye = jnp.where(row == col, 1.0, 0.0)
    n1 = [jnp.where(same_block(8), l, 0.0) for l in lmats]
    n2 = [_bdot(a, a) for a in n1]
    t = [eye - a for a in n1]
    t = [ti + _bdot(ti, b) for ti, b in zip(t, n2)]
    n4 = [_bdot(b, b) for b in n2]
    t = [ti + _bdot(ti, b) for ti, b in zip(t, n4)]
    size = 16
    while size <= c:
        cross = same_block(size) & jnp.logical_not(same_block(size // 2))
        tc = [_bdot(ti, jnp.where(cross, l, 0.0)) for ti, l in zip(t, lmats)]
        t = [ti - _bdot(tci, ti) for ti, tci in zip(t, tc)]
        size *= 2
    return t


def _dn_core_kernel(alog_ref, dtb_ref,
                    q_ref, k_ref, v_ref, z_ref, gate_ref,
                    cwq_ref, cwk_ref, cwv_ref, nw_ref,
                    cbq_ref, cbk_ref, cbv_ref, s0_ref,
                    o_ref, sout_ref,
                    xq_ref, xk_ref, xv_ref, s_ref,
                    *, tb, t_valid, nt, npair):
    hg = pl.program_id(1)
    t = pl.program_id(2)
    c = DN_CHUNK
    nchunk = tb // c
    hd = DN_HEAD
    pad = 8
    tail = CONV_W - 1

    @pl.when(t == 0)
    def _():
        xq_ref[pad - tail:pad, :] = cbq_ref[0]
        xk_ref[pad - tail:pad, :] = cbk_ref[0]
        xv_ref[pad - tail:pad, :] = cbv_ref[0]
        s_ref[...] = s0_ref[0]

    xq_ref[pad:pad + tb, :] = q_ref[0]
    xk_ref[pad:pad + tb, :] = k_ref[0]
    xv_ref[pad:pad + tb, :] = v_ref[0]

    def conv(x_ref, cw_ref):
        acc = x_ref[pad - tail:pad - tail + tb, :] * cw_ref[0:1, :]
        for tap in range(1, CONV_W):
            acc = acc + x_ref[pad - tail + tap:pad - tail + tap + tb, :] * cw_ref[tap:tap + 1, :]
        return _silu(acc)

    q_all = conv(xq_ref, cwq_ref)
    k_all = conv(xk_ref, cwk_ref)
    v_all = conv(xv_ref, cwv_ref)

    xq_ref[pad - tail:pad, :] = xq_ref[pad + tb - tail:pad + tb, :]
    xk_ref[pad - tail:pad, :] = xk_ref[pad + tb - tail:pad + tb, :]
    xv_ref[pad - tail:pad, :] = xv_ref[pad + tb - tail:pad + tb, :]

    qs, ks = [], []
    for p in range(npair):
        qp = q_all[:, p * hd:(p + 1) * hd]
        kp = k_all[:, p * hd:(p + 1) * hd]
        qs.append(qp * lax.rsqrt(jnp.sum(qp * qp, axis=-1, keepdims=True) + EPS) * (hd ** -0.5))
        ks.append(kp * lax.rsqrt(jnp.sum(kp * kp, axis=-1, keepdims=True) + EPS))

    row = lax.broadcasted_iota(jnp.int32, (c, c), 0)
    col = lax.broadcasted_iota(jnp.int32, (c, c), 1)
    eye = row == col
    tril = row >= col
    strict = row > col
    lane_t = lax.broadcasted_iota(jnp.int32, (1, c), 1)

    qk_idx = [(ci, p) for ci in range(nchunk) for p in range(npair)]
    qcs = {(ci, p): qs[p][ci * c:(ci + 1) * c] for ci, p in qk_idx}
    kcs = {(ci, p): ks[p][ci * c:(ci + 1) * c] for ci, p in qk_idx}
    qks = {key: _bdot_nt(jnp.concatenate([qcs[key], kcs[key]], axis=0), kcs[key]) for key in qk_idx}
    systems = [(ci, p, r) for ci in range(nchunk) for p in range(npair) for r in range(2)]
    lmats, a_intras, beta_cols, egcs, kdecs, sdecs = [], [], [], [], [], []
    for ci, p, r in systems:
        head = 2 * (hg * npair + p) + r
        gates = gate_ref[0, p, ci]
        valid = (t * tb + ci * c + lane_t) < t_valid
        beta_row = jnp.where(valid, jax.nn.sigmoid(gates[r:r + 1, :]), 0.0)
        a_scale = -jnp.exp(jnp.full((1, c), alog_ref[head], F32))
        g_row = jnp.where(valid, a_scale * _softplus(gates[2 + r:3 + r, :] + dtb_ref[head]), 0.0)
        gc_col = jnp.sum(jnp.where(tril, g_row, 0.0), axis=1, keepdims=True)
        gc_row = jnp.sum(jnp.where(eye, gc_col, 0.0), axis=0, keepdims=True)
        beta_col = jnp.sum(jnp.where(eye, beta_row, 0.0), axis=1, keepdims=True)
        decay = jnp.exp(jnp.where(tril, gc_col - gc_row, NEG_INF))
        qkt, kkt = qks[ci, p][:c], qks[ci, p][c:]
        lmats.append(jnp.where(strict, kkt * beta_col * decay, 0.0))
        a_intras.append(jnp.where(tril, qkt * decay, 0.0))
        g_last = gc_col[c - 1:c, :]
        beta_cols.append(beta_col)
        egcs.append(jnp.exp(gc_col))
        kdecs.append(kcs[ci, p] * jnp.exp(g_last - gc_col))
        sdecs.append(jnp.exp(g_last))
    tinvs = _unit_lower_inverses(lmats)
    uws = []
    for i, (ci, p, r) in enumerate(systems):
        vc = v_all[ci * c:(ci + 1) * c, (2 * p + r) * hd:(2 * p + r + 1) * hd]
        rhs = jnp.concatenate([vc * beta_cols[i], kcs[ci, p] * (beta_cols[i] * egcs[i])], axis=1)
        uws.append(_bdot(tinvs[i], rhs))

    nh = 2 * npair
    states = [s_ref[h] for h in range(nh)]
    lhs = [jnp.concatenate([uws[i][:, hd:], qcs[ci, p] * egcs[i]], axis=0)
           for i, (ci, p, r) in enumerate(systems)]
    kdts = [kd.T for kd in kdecs]
    outs = []
    for ci in range(nchunk):
        ids = range(ci * nh, (ci + 1) * nh)
        ws_qs = [_bdot(lhs[i], states[i - ci * nh]) for i in ids]
        v_new = [uws[i][:, :hd] - x[:c] for i, x in zip(ids, ws_qs)]
        outs += [x[c:] + _bdot(a_intras[i], vn) for i, x, vn in zip(ids, ws_qs, v_new)]
        states = [states[i - ci * nh] * sdecs[i] + _bdot(kdts[i], vn) for i, vn in zip(ids, v_new)]
    for h in range(nh):
        s_ref[h] = states[h]

    for i, (ci, p, r) in enumerate(systems):
        o = outs[i]
        cols = slice((2 * p + r) * hd, (2 * p + r + 1) * hd)
        zc = z_ref[0, ci * c:(ci + 1) * c, cols]
        o = o * lax.rsqrt(jnp.mean(o * o, axis=-1, keepdims=True) + EPS) * nw_ref[...] * _silu(zc)
        o_ref[0, ci * c:(ci + 1) * c, cols] = o.astype(o_ref.dtype)

    @pl.when(t == nt - 1)
    def _():
        sout_ref[0] = s_ref[...]


def dn_core(proj, gates_t, conv_w, a_log, dt_bias, norm_w, conv_buf, s0, *, t_valid, tb, npair):
    b, t, _ = proj.shape
    nt = t // tb
    ng = DN_QK_HEADS // npair
    wq = DN_HEAD * npair
    wv = 2 * wq
    kern = functools.partial(_dn_core_kernel, tb=tb, t_valid=t_valid, nt=nt, npair=npair)
    smem = pl.BlockSpec(memory_space=pltpu.SMEM)
    k_blk0 = DN_KEY_DIM // wq
    v_blk0 = 2 * DN_KEY_DIM // wv
    z_blk0 = DN_CONV_DIM // wv
    in_specs = [
        smem, smem,
        pl.BlockSpec((1, tb, wq), lambda bi, h, ti: (bi, ti, h)),
        pl.BlockSpec((1, tb, wq), lambda bi, h, ti: (bi, ti, k_blk0 + h)),
        pl.BlockSpec((1, tb, wv), lambda bi, h, ti: (bi, ti, v_blk0 + h)),
        pl.BlockSpec((1, tb, wv), lambda bi, h, ti: (bi, ti, z_blk0 + h)),
        pl.BlockSpec((1, npair, tb // DN_CHUNK, 4, DN_CHUNK), lambda bi, h, ti: (bi, h, ti, 0, 0)),
        pl.BlockSpec((CONV_W, wq), lambda bi, h, ti: (0, h)),
        pl.BlockSpec((CONV_W, wq), lambda bi, h, ti: (0, k_blk0 + h)),
        pl.BlockSpec((CONV_W, wv), lambda bi, h, ti: (0, v_blk0 + h)),
        pl.BlockSpec((1, DN_HEAD), lambda bi, h, ti: (0, 0)),
        pl.BlockSpec((1, CONV_W - 1, wq), lambda bi, h, ti: (bi, 0, h)),
        pl.BlockSpec((1, CONV_W - 1, wq), lambda bi, h, ti: (bi, 0, k_blk0 + h)),
        pl.BlockSpec((1, CONV_W - 1, wv), lambda bi, h, ti: (bi, 0, v_blk0 + h)),
        pl.BlockSpec((1, 2 * npair, DN_HEAD, DN_HEAD), lambda bi, h, ti: (bi, h, 0, 0)),
    ]
    out_specs = [
        pl.BlockSpec((1, tb, wv), lambda bi, h, ti: (bi, ti, h)),
        pl.BlockSpec((1, 2 * npair, DN_HEAD, DN_HEAD), lambda bi, h, ti: (bi, h, 0, 0)),
    ]
    return pl.pallas_call(
        kern,
        out_shape=[jax.ShapeDtypeStruct((b, t, DN_VAL_DIM), BF16),
                   jax.ShapeDtypeStruct((b, DN_V_HEADS, DN_HEAD, DN_HEAD), F32)],
        grid=(b, ng, nt),
        in_specs=in_specs,
        out_specs=out_specs,
        scratch_shapes=[pltpu.VMEM((tb + 8, wq), F32), pltpu.VMEM((tb + 8, wq), F32),
                        pltpu.VMEM((tb + 8, wv), F32), pltpu.VMEM((2 * npair, DN_HEAD, DN_HEAD), F32)],
        compiler_params=_cparams("parallel", "parallel", "arbitrary"),
        name="dn_core",
    )(a_log, dt_bias, proj, proj, proj, proj, gates_t,
      conv_w, conv_w, conv_w, norm_w.reshape(1, DN_HEAD),
      conv_buf, conv_buf, conv_buf, s0)


def deltanet_layer(x, ln_g, w_main, w_gate, conv_w, a_log, dt_bias, norm_w, w_out, conv_buf, s0):
    b, t, d = x.shape
    x2 = x.reshape(b * t, d)
    proj = norm_matmul(x2, ln_g, w_main).reshape(b, t, -1)
    gates = norm_matmul(x2, ln_g, w_gate).reshape(b, t, -1)[:, :, :2 * DN_V_HEADS]
    tail = CONV_W - 1
    if t >= tail:
        new_buf = proj[:, t - tail:, :DN_CONV_DIM]
    else:
        new_buf = jnp.concatenate([conv_buf, proj[:, :, :DN_CONV_DIM]], axis=1)[:, -tail:]
    tp = -(-t // DN_CHUNK) * DN_CHUNK
    if tp != t:
        proj = jnp.pad(proj, ((0, 0), (0, tp - t), (0, 0)))
        gates = jnp.pad(gates, ((0, 0), (0, tp - t), (0, 0)))
    g5 = gates.reshape(b, tp // DN_CHUNK, DN_CHUNK, 2, DN_QK_HEADS, 2)
    gates_t = jnp.transpose(g5, (0, 4, 1, 3, 5, 2)).reshape(b, DN_QK_HEADS, tp // DN_CHUNK, 4, DN_CHUNK)
    tb = _pick_tile(tp, 256)
    o, s_new = dn_core(proj, gates_t, conv_w, a_log, dt_bias, norm_w, conv_buf, s0,
                       t_valid=t, tb=tb, npair=DN_PAIRS_PER_STEP)
    o = o[:, :t].reshape(b * t, DN_VAL_DIM)
    y = matmul_residual(o, w_out, x2)
    return y.reshape(b, t, d), new_buf, s_new


def mlp_layer(x, ln_g, w_up, w_down):
    b, t, d = x.shape
    x2 = x.reshape(b * t, d)
    hid = norm_matmul(x2, ln_g, w_up, act=True, out_dtype=BF16)
    return matmul_residual(hid, w_down, x2).reshape(b, t, d)


def _rel_bias_table_kernel(rbt_ref, o_ref, *, r0, step, ltab):
    rel = (r0 + pl.program_id(0)) + step * lax.broadcasted_iota(jnp.int32, (1, ltab), 1)
    n = jnp.maximum(rel, 0)
    max_exact = REL_BUCKETS // 2
    nf = jnp.maximum(n, max_exact).astype(F32)
    large = max_exact + (jnp.log(nf / max_exact) / math.log(REL_MAX_DIST / max_exact)
                         * (REL_BUCKETS - max_exact)).astype(jnp.int32)
    large = jnp.minimum(large, REL_BUCKETS - 1)
    bucket = jnp.where(n < max_exact, n, large)
    rbt = rbt_ref[...]
    out = jnp.zeros((MB_HEADS, ltab), F32)
    for bkt in range(REL_BUCKETS):
        out = jnp.where(bucket == bkt, rbt[:, bkt:bkt + 1], out)
    o_ref[0] = out


def rel_bias_table(rel_bias, *, r0, step, nrow, ltab):
    return pl.pallas_call(
        functools.partial(_rel_bias_table_kernel, r0=r0, step=step, ltab=ltab),
        out_shape=jax.ShapeDtypeStruct((nrow, MB_HEADS, ltab), F32),
        grid=(nrow,),
        in_specs=[pl.BlockSpec((MB_HEADS, REL_BUCKETS), lambda i: (0, 0))],
        out_specs=pl.BlockSpec((1, MB_HEADS, ltab), lambda i: (i, 0, 0)),
        compiler_params=_cparams("arbitrary"),
        name="rel_bias_table",
    )(rel_bias.T)


def _qk_norm_kernel(q_ref, k_ref, v_ref, qg_ref, kg_ref, kn_ref, qa_ref, ka_ref, va_ref, km_ref,
                    *, v_transposed):
    qg = qg_ref[...]
    kg = kg_ref[...]
    va_ref[0] = (v_ref[0].T if v_transposed else v_ref[0]).astype(va_ref.dtype)
    for h in range(MB_HEADS):
        sl = slice(h * MB_HEAD_DIM, (h + 1) * MB_HEAD_DIM)
        qh = q_ref[0, :, sl]
        kh = k_ref[0, :, sl]
        qn = qh * lax.rsqrt(jnp.mean(qh * qh, axis=-1, keepdims=True) + EPS) * qg
        kn = kh * lax.rsqrt(jnp.mean(kh * kh, axis=-1, keepdims=True) + EPS) * kg
        kn_ref[0, :, sl] = kn
        qa_ref[0, :, sl] = qn.astype(qa_ref.dtype)
        ka_ref[0, :, sl] = kn.astype(ka_ref.dtype)
        km_ref[0, 0, :, sl] = jnp.sum(kn, axis=0, keepdims=True) / MB_BLOCK


def qk_norm(qkv, q_gain, k_gain, *, rows, attn_dtype, v_transposed):
    b, t, _ = qkv.shape
    nb = t // rows
    d = MB_HEADS * MB_HEAD_DIM
    row_spec = [pl.BlockSpec((1, rows, d), functools.partial(lambda bi, i, c: (bi, i, c), c=c))
                for c in range(3)]
    gain_spec = pl.BlockSpec((1, MB_HEAD_DIM), lambda bi, i: (0, 0))
    v_shape, v_spec = (b, t, d), row_spec[0]
    if v_transposed:
        v_shape, v_spec = (b, d, t), pl.BlockSpec((1, d, rows), lambda bi, i: (bi, 0, i))
    return pl.pallas_call(
        functools.partial(_qk_norm_kernel, v_transposed=v_transposed),
        out_shape=[jax.ShapeDtypeStruct((b, t, d), F32)]
        + [jax.ShapeDtypeStruct((b, t, d), attn_dtype)] * 2
        + [jax.ShapeDtypeStruct(v_shape, attn_dtype), jax.ShapeDtypeStruct((b, nb, 1, d), F32)],
        grid=(b, nb),
        in_specs=row_spec + [gain_spec, gain_spec],
        out_specs=[row_spec[0]] * 3 + [v_spec, pl.BlockSpec((1, 1, 1, d), lambda bi, i: (bi, i, 0, 0))],
        compiler_params=_cparams("parallel", "parallel"),
        name="qk_norm",
    )(qkv, qkv, qkv, q_gain.reshape(1, -1), k_gain.reshape(1, -1))


def _block_scores(q, km):
    nb = km.shape[0]
    km = jnp.concatenate([km, jnp.zeros((128 - nb, km.shape[1]), F32)], axis=0)
    return _bdot_nt(q, km)


def _block_ranks(scores, ncand, nb):
    lane = lax.broadcasted_iota(jnp.int32, scores.shape, 1)
    sm = jnp.where(lane < ncand, scores, NEG_INF)
    cnt = jnp.zeros(scores.shape, jnp.int32)
    for m in range(nb):
        col = sm[:, m:m + 1]
        better = (col > sm) | ((col == sm) & (lane > m))
        cnt = cnt + jnp.where(better & (m < ncand), 1, 0)
    return cnt


def _moba_prompt_kernel(tbl_ref, q_ref, k_ref, vt_ref, km_ref, o_ref, bias_ref, wide_ref, pen_ref,
                        *, nblk, nhead):
    b = pl.program_id(1)
    qi = pl.program_id(2)
    blk = MB_BLOCK
    hd = MB_HEAD_DIM
    scale = hd ** -0.5
    heads = range(nhead)

    @pl.when(b == 0)
    def _():
        start = pl.multiple_of(qi * blk, blk)
        for h in heads:
            u = tbl_ref[h, :, pl.ds(start, 2 * blk)]
            wide_ref[...] = jnp.broadcast_to(u, (blk, 2 * blk))
            tile = pltpu.roll(wide_ref[...], blk + 1, 1, stride=1, stride_axis=0)
            bias_ref[h, qi] = tile[:, :blk]

    def cols(h):
        return slice(h * hd, (h + 1) * hd)

    qs = [q_ref[0, :, cols(h)] for h in heads]
    blk_id = lax.broadcasted_iota(jnp.int32, (nblk, blk), 0)
    for h in heads:
        sm = jnp.where(blk_id < qi, _bdot_nt(km_ref[0, :, 0, cols(h)], qs[h]), NEG_INF)
        cnt = jnp.zeros((nblk, blk), jnp.int32)
        for m in range(nblk):
            rowm = sm[m:m + 1, :]
            better = (rowm > sm) | ((rowm == sm) & (blk_id > m))
            cnt = cnt + jnp.where(better & (m < qi), 1, 0)
        pen_ref[h] = jnp.where((blk_id < qi) & (cnt < MB_TOPK), 0.0, NEG_INF)

    key = lax.broadcasted_iota(jnp.int32, (blk, blk), 0)
    qry = lax.broadcasted_iota(jnp.int32, (blk, blk), 1)
    own0 = pl.multiple_of(qi * blk, blk)
    ss = [_bdot_nt(k_ref[0, pl.ds(own0, blk), cols(h)], qs[h]) for h in heads]
    ss = [jnp.where(key <= qry, ss[h] * scale + bias_ref[h, 0], NEG_INF) for h in heads]
    m0 = [jnp.max(x, axis=0, keepdims=True) for x in ss]
    ps = [jnp.exp(x - mm) for x, mm in zip(ss, m0)]
    l0 = [jnp.sum(p, axis=0, keepdims=True) for p in ps]
    acc0 = [_bdot(vt_ref[0, cols(h), pl.ds(own0, blk)], ps[h]) for h in heads]

    def body(j, carry):
        ms, ls, accs = carry
        n0 = 2 * j
        r0 = pl.multiple_of(n0 * blk, 2 * blk)
        d1 = jnp.maximum(qi - n0 - 1, 0)
        ss = [_bdot_nt(k_ref[0, pl.ds(r0, 2 * blk), cols(h)], qs[h]) for h in heads]
        add = [jnp.concatenate([bias_ref[h, qi - n0] + pen_ref[h, pl.ds(n0, 1), :],
                                bias_ref[h, d1] + pen_ref[h, pl.ds(n0 + 1, 1), :]], axis=0) for h in heads]
        ss = [ss[h] * scale + add[h] for h in heads]
        m_new = [jnp.maximum(ms[h], jnp.max(ss[h], axis=0, keepdims=True)) for h in heads]
        alpha = [jnp.exp(ms[h] - m_new[h]) for h in heads]
        ps = [jnp.exp(ss[h] - m_new[h]) for h in heads]
        ls = tuple(ls[h] * alpha[h] + jnp.sum(ps[h], axis=0, keepdims=True) for h in heads)
        pv = [_bdot(vt_ref[0, cols(h), pl.ds(r0, 2 * blk)], ps[h]) for h in heads]
        accs = tuple(accs[h] * alpha[h] + pv[h] for h in heads)
        return tuple(m_new), ls, accs

    ms, ls, accs = lax.fori_loop(0, (qi + 1) // 2, body, (tuple(m0), tuple(l0), tuple(acc0)))
    for h in heads:
        o_ref[0, :, cols(h)] = (accs[h] / ls[h]).T.astype(o_ref.dtype)


def moba_prompt_attention(q, k, vt, kmean, tbl):
    b, s, d = q.shape
    nblk = s // MB_BLOCK
    assert nblk % 2 == 0
    nh = MB_HEADS_PER_STEP
    w = nh * MB_HEAD_DIM
    return pl.pallas_call(
        functools.partial(_moba_prompt_kernel, nblk=nblk, nhead=nh),
        out_shape=jax.ShapeDtypeStruct((b, s, d), BF16),
        grid=(MB_HEADS // nh, b, nblk),
        in_specs=[pl.BlockSpec((nh, 1, s + MB_BLOCK), lambda h, bi, i: (h, 0, 0)),
                  pl.BlockSpec((1, MB_BLOCK, w), lambda h, bi, i: (bi, i, h)),
                  pl.BlockSpec((1, s, w), lambda h, bi, i: (bi, 0, h)),
                  pl.BlockSpec((1, w, s), lambda h, bi, i: (bi, h, 0)),
                  pl.BlockSpec((1, nblk, 1, w), lambda h, bi, i: (bi, 0, 0, h))],
        out_specs=pl.BlockSpec((1, MB_BLOCK, w), lambda h, bi, i: (bi, i, h)),
        scratch_shapes=[pltpu.VMEM((nh, nblk, MB_BLOCK, MB_BLOCK), F32),
                        pltpu.VMEM((MB_BLOCK, 2 * MB_BLOCK), F32),
                        pltpu.VMEM((nh, nblk, MB_BLOCK), F32)],
        compiler_params=_cparams("parallel", "arbitrary", "arbitrary"),
        name="moba_prompt_attention",
    )(tbl, q, k, vt, kmean)


def moba_prompt_layer(x, ln_g, w_qkv, q_gain, k_gain, w_out, tbl):
    b, s, d = x.shape
    assert s % MB_BLOCK == 0
    x2 = x.reshape(b * s, d)
    qkv = norm_matmul(x2, ln_g, w_qkv).reshape(b, s, -1)
    kn, qa, ka, vt, kmean = qk_norm(qkv, q_gain, k_gain, rows=MB_BLOCK, attn_dtype=BF16, v_transposed=True)
    o = moba_prompt_attention(qa, ka, vt, kmean, tbl)
    y = matmul_residual(o.reshape(b * s, d), w_out, x2).reshape(b, s, d)
    new_k = kn.reshape(b, s, MB_HEADS, MB_HEAD_DIM)
    new_v = qkv[:, :, 2 * d:].reshape(b, s, MB_HEADS, MB_HEAD_DIM)
    return y, new_k, new_v


PAGES_PER_MEAN_STEP = 4


def _page_block_mean_kernel(pt_ref, *refs):
    del pt_ref
    o_ref = refs[-1]
    for blk in range(PAGES_PER_MEAN_STEP // 2):
        a_ref, b_ref = refs[2 * blk], refs[2 * blk + 1]
        o_ref[0, blk] = (jnp.sum(a_ref[0, 0], axis=0) + jnp.sum(b_ref[0, 0], axis=0)) / MB_BLOCK


def page_block_means(cache_k, layer, pt_flat, bsz, n_pages):
    nb = n_pages // 2
    npg = PAGES_PER_MEAN_STEP
    assert n_pages % npg == 0
    page_shape = cache_k.shape[2:]

    def page_spec(off):
        return pl.BlockSpec((1, 1) + page_shape,
                            lambda bi, n, pt: (layer, pt[bi * n_pages + npg * n + off], 0, 0, 0))

    return pl.pallas_call(
        _page_block_mean_kernel,
        out_shape=jax.ShapeDtypeStruct((bsz, nb) + page_shape[1:], F32),
        grid_spec=pltpu.PrefetchScalarGridSpec(
            num_scalar_prefetch=1, grid=(bsz, n_pages // npg),
            in_specs=[page_spec(off) for off in range(npg)],
            out_specs=pl.BlockSpec((1, npg // 2) + page_shape[1:], lambda bi, n, pt: (bi, n, 0, 0))),
        compiler_params=_cparams("parallel", "parallel"),
        name="page_block_means",
    )(pt_flat, *([cache_k] * npg))


def _sample_select_kernel(q_ref, km_ref, o_ref, *, past, nb):
    lane = lax.broadcasted_iota(jnp.int32, (8, 128), 1)
    ncand = (past + lax.broadcasted_iota(jnp.int32, (8, 1), 0)) // MB_BLOCK
    cand = lane < ncand
    t = q_ref.shape[1]
    for h in range(MB_HEADS):
        sl = slice(h * MB_HEAD_DIM, (h + 1) * MB_HEAD_DIM)
        q = jnp.concatenate([q_ref[0, :, sl], jnp.zeros((8 - t, MB_HEAD_DIM), F32)], axis=0)
        rank = _block_ranks(_block_scores(q, km_ref[0, :, 0, sl]), ncand, nb)
        out = jnp.zeros((8, 128), jnp.int32)
        for r in range(MB_TOPK):
            idx = jnp.sum(jnp.where(cand & (rank == r), lane, 0), axis=1, keepdims=True)
            out = jnp.where(lane == r, idx, out)
        o_ref[0, h] = out


def sample_select(qn, kmean, past):
    b, t, d = qn.shape
    nb = kmean.shape[1]
    assert t <= 8 and nb <= 128
    return pl.pallas_call(
        functools.partial(_sample_select_kernel, past=past, nb=nb),
        out_shape=jax.ShapeDtypeStruct((b, MB_HEADS, 8, 128), jnp.int32),
        grid=(b,),
        in_specs=[pl.BlockSpec((1, t, d), lambda bi: (bi, 0, 0)),
                  pl.BlockSpec((1, nb, 1, d), lambda bi: (bi, 0, 0, 0))],
        out_specs=pl.BlockSpec((1, MB_HEADS, 8, 128), lambda bi: (bi, 0, 0, 0)),
        compiler_params=_cparams("parallel"),
        name="sample_select",
    )(qn, kmean)


def _sample_attn_kernel(idx_ref, pt_ref, q_ref, kn_ref, vn_ref, tbl_ref, ck_ref, cv_ref,
                        o_ref, kbuf, vbuf, sem, *, layer, n_pages, nq):
    g = pl.program_id(0)
    nsteps = pl.num_programs(0)
    slot = g % 2
    per_q = MB_TOPK * 2
    past = n_pages * PAGE_SIZE
    scale = MB_HEAD_DIM ** -0.5

    def logical_page(gg, t, r, pg):
        return idx_ref[(gg * nq + t) * MB_TOPK + r] * 2 + pg

    def page_copies(gg, sl):
        b = gg // MB_HEADS
        h = gg % MB_HEADS
        out = []
        for t in range(nq):
            for r in range(MB_TOPK):
                for pg in range(2):
                    i = (t * MB_TOPK + r) * 2 + pg
                    phys = pt_ref[b * n_pages + logical_page(gg, t, r, pg)]
                    out.append(pltpu.make_async_copy(ck_ref.at[layer, phys, :, h, :], kbuf.at[sl, i], sem.at[sl, 0]))
                    out.append(pltpu.make_async_copy(cv_ref.at[layer, phys, :, h, :], vbuf.at[sl, i], sem.at[sl, 1]))
        return out

    @pl.when(g == 0)
    def _():
        for cp in page_copies(g, slot):
            cp.start()

    @pl.when(g + 1 < nsteps)
    def _():
        for cp in page_copies(g + 1, 1 - slot):
            cp.start()

    for cp in page_copies(g, slot):
        cp.wait()

    zpad = jnp.zeros((PAGE_SIZE - nq, MB_HEAD_DIM), F32)
    kown = jnp.concatenate([kn_ref[0], zpad], axis=0)
    vown = jnp.concatenate([vn_ref[0], zpad], axis=0)
    lane = lax.broadcasted_iota(jnp.int32, (1, PAGE_SIZE), 1)
    own_bias0 = pl.multiple_of(n_pages * PAGE_SIZE, PAGE_SIZE)
    for t in range(nq):
        q8 = jnp.broadcast_to(q_ref[0, t:t + 1, :], (8, MB_HEAD_DIM))
        kt = kbuf[slot, t * per_q:(t + 1) * per_q].reshape(per_q * PAGE_SIZE, MB_HEAD_DIM)
        vt = vbuf[slot, t * per_q:(t + 1) * per_q].reshape(per_q * PAGE_SIZE, MB_HEAD_DIM)
        bias, oks = [], []
        for r in range(MB_TOPK):
            ok = r < min((past + t) // MB_BLOCK, MB_TOPK)
            for pg in range(2):
                col0 = pl.multiple_of(logical_page(g, t, r, pg) * PAGE_SIZE, PAGE_SIZE)
                bias.append(tbl_ref[t, 0, :, pl.ds(col0, PAGE_SIZE)])
                oks.append(jnp.full((1, PAGE_SIZE), ok))
        bias.append(tbl_ref[t, 0, :, pl.ds(own_bias0, PAGE_SIZE)])
        oks.append(lane <= t)
        bias = jnp.concatenate(bias, axis=1)
        okm = jnp.concatenate(oks, axis=1)
        sc = jnp.concatenate([_bdot_nt(q8, kt)[0:1], _bdot_nt(q8, kown)[0:1]], axis=1) * scale + bias
        sc = jnp.where(okm, sc, NEG_INF)
        p = jnp.exp(sc - jnp.max(sc, axis=1, keepdims=True))
        l = jnp.sum(p, axis=1, keepdims=True)
        p8 = jnp.broadcast_to(p, (8, p.shape[1]))
        acc = _bdot(p8[:, :per_q * PAGE_SIZE], vt)[0:1] + _bdot(p8[:, per_q * PAGE_SIZE:], vown)[0:1]
        o_ref[0, t:t + 1, :] = acc / l


def sample_attention(qn, kn, vn, cache_k, cache_v, layer, idx_flat, pt_flat, tbl, n_pages):
    b, t, d = qn.shape
    hd = MB_HEAD_DIM
    ngather = t * MB_TOPK * 2
    row_spec = pl.BlockSpec((1, t, hd), lambda g, idx, pt: (g // MB_HEADS, 0, g % MB_HEADS))
    in_specs = [
        row_spec, row_spec, row_spec,
        pl.BlockSpec((t, 1, 1, tbl.shape[-1]), lambda g, idx, pt: (0, g % MB_HEADS, 0, 0)),
        pl.BlockSpec(memory_space=pl.ANY), pl.BlockSpec(memory_space=pl.ANY),
    ]
    return pl.pallas_call(
        functools.partial(_sample_attn_kernel, layer=layer, n_pages=n_pages, nq=t),
        out_shape=jax.ShapeDtypeStruct((b, t, d), F32),
        grid_spec=pltpu.PrefetchScalarGridSpec(
            num_scalar_prefetch=2, grid=(b * MB_HEADS,),
            in_specs=in_specs,
            out_specs=row_spec,
            scratch_shapes=[pltpu.VMEM((2, ngather, PAGE_SIZE, hd), F32),
                            pltpu.VMEM((2, ngather, PAGE_SIZE, hd), F32),
                            pltpu.SemaphoreType.DMA((2, 2))]),
        compiler_params=_cparams("arbitrary"),
        name="sample_attention",
    )(idx_flat, pt_flat, qn, kn, vn, tbl, cache_k, cache_v)


def moba_sample_layer(x, ln_g, w_qkv, q_gain, k_gain, w_out, cache_k, cache_v, layer, page_table, tbl):
    b, t, d = x.shape
    n_pages = page_table.shape[1]
    past = n_pages * PAGE_SIZE
    assert n_pages % 2 == 0 and t <= 8
    x2 = x.reshape(b * t, d)
    qkv = norm_matmul(x2, ln_g, w_qkv).reshape(b, t, -1)
    kn, qn, _, vn, _ = qk_norm(qkv, q_gain, k_gain, rows=t, attn_dtype=F32, v_transposed=False)
    pt_flat = page_table.reshape(-1)
    kmean = page_block_means(cache_k, layer, pt_flat, b, n_pages).reshape(b, n_pages // 2, 1, d)
    idx = sample_select(qn, kmean, past)[:, :, :t, :MB_TOPK].reshape(-1)
    o = sample_attention(qn, kn, vn, cache_k, cache_v, layer, idx, pt_flat, tbl, n_pages)
    y = matmul_residual(o.reshape(b * t, d).astype(BF16), w_out, x2).reshape(b, t, d)
    new_k = kn.reshape(b, t, MB_HEADS, MB_HEAD_DIM)
    new_v = qkv[:, :, 2 * d:].reshape(b, t, MB_HEADS, MB_HEAD_DIM)
    return y, new_k, new_v


def kernel(x_prompt, x_sample, state_delta, state_conv, cache_k, cache_v, page_table, rel_bias,
           ln_mix, ln_mlp, dn_w_in, dn_conv_w, dn_a_log, dn_dt_bias, dn_norm_w, dn_w_out,
           mb_w_qkv, mb_q_norm, mb_k_norm, mb_w_out, mlp_w_up, mlp_w_down):
    depth = ln_mix.shape[0]
    bp, sp, _ = x_prompt.shape
    bd, td, _ = x_sample.shape
    n_pages = page_table.shape[1]
    past = n_pages * PAGE_SIZE

    tbl_p = rel_bias_table(rel_bias, r0=1 - MB_BLOCK, step=1, nrow=1, ltab=sp + MB_BLOCK)
    tbl_p = tbl_p.reshape(MB_HEADS, 1, sp + MB_BLOCK)
    tbl_s = rel_bias_table(rel_bias, r0=past, step=-1, nrow=td, ltab=past + PAGE_SIZE)
    tbl_s = tbl_s.reshape(td, MB_HEADS, 1, past + PAGE_SIZE)

    xp, xd = x_prompt, x_sample
    sdp, scp, sds, scs, kps, vps, kds, vds = [], [], [], [], [], [], [], []
    for i in range(depth):
        j = i // 2
        if i % 2 == 0:
            w_main = cast_weight(dn_w_in, j, 0, DN_CONV_DIM + DN_VAL_DIM)
            w_gate = cast_weight(dn_w_in, j, DN_CONV_DIM + DN_VAL_DIM, 128)
            w_out = cast_weight(dn_w_out, j)
            args = (w_main, w_gate, dn_conv_w[j], dn_a_log[j], dn_dt_bias[j], dn_norm_w[j], w_out)
            buf0 = jnp.zeros((bp, CONV_W - 1, DN_CONV_DIM), F32)
            s0 = jnp.zeros((bp, DN_V_HEADS, DN_HEAD, DN_HEAD), F32)
            xp, buf_p, s_p = deltanet_layer(xp, ln_mix[i], *args, buf0, s0)
            xd, buf_d, s_d = deltanet_layer(xd, ln_mix[i], *args, state_conv[j], state_delta[j])
            sdp.append(s_p)
            scp.append(buf_p)
            sds.append(s_d)
            scs.append(buf_d)
        else:
            w_qkv = cast_weight(mb_w_qkv, j)
            w_out = cast_weight(mb_w_out, j)
            xp, kp, vp = moba_prompt_layer(xp, ln_mix[i], w_qkv, mb_q_norm[j], mb_k_norm[j], w_out, tbl_p)
            xd, kd, vd = moba_sample_layer(xd, ln_mix[i], w_qkv, mb_q_norm[j], mb_k_norm[j], w_out,
                                           cache_k, cache_v, j, page_table, tbl_s)
            kps.append(kp)
            vps.append(vp)
            kds.append(kd)
            vds.append(vd)
        w_up = cast_weight(mlp_w_up, i)
        w_down = cast_weight(mlp_w_down, i)
        xp = mlp_layer(xp, ln_mlp[i], w_up, w_down)
        xd = mlp_layer(xd, ln_mlp[i], w_up, w_down)
    return (xp, xd, jnp.stack(sdp), jnp.stack(scp), jnp.stack(kps), jnp.stack(vps),
            jnp.stack(sds), jnp.stack(scs), jnp.stack(kds), jnp.stack(vds))
```

```python
import functools
import math

import jax
import jax.numpy as jnp
from jax import lax
from jax.experimental import pallas as pl
from jax.experimental.pallas import tpu as pltpu

F32 = jnp.float32
BF16 = jnp.bfloat16

D_MODEL = 2048
DN_QK_HEADS = 16
DN_V_HEADS = 32
DN_HEAD = 128
DN_KEY_DIM = DN_QK_HEADS * DN_HEAD
DN_VAL_DIM = DN_V_HEADS * DN_HEAD
DN_CONV_DIM = 2 * DN_KEY_DIM + DN_VAL_DIM
CONV_W = 4
DN_CHUNK = 64
DN_PAIRS_PER_STEP = 4
MB_HEADS = 16
MB_HEAD_DIM = 128
MB_BLOCK = 256
MB_TOPK = 3
MB_HEADS_PER_STEP = 4
PAGE_SIZE = 128
REL_BUCKETS = 32
REL_MAX_DIST = 4096
EPS = 1e-6
NEG_INF = -1e30

VMEM_LIMIT_BYTES = 56 * 1024 * 1024


def _cparams(*sem):
    return pltpu.CompilerParams(dimension_semantics=sem, vmem_limit_bytes=VMEM_LIMIT_BYTES)


def _pick_tile(n, pref):
    if n <= pref:
        return n
    t = pref
    while n % t:
        t //= 2
    return t


def _cast_kernel(w_ref, o_ref, *, valid_cols):
    w = w_ref[0]
    if valid_cols < w.shape[1]:
        lane = lax.broadcasted_iota(jnp.int32, w.shape, 1)
        w = jnp.where(lane < valid_cols, w, 0.0)
    o_ref[...] = w.astype(o_ref.dtype)


def cast_weight(w, layer, col0=0, ncols=None):
    _, k, n = w.shape
    ncols = n if ncols is None else ncols
    tk = _pick_tile(k, 512)
    tn = _pick_tile(ncols, 2048)
    assert col0 % tn == 0 and (col0 + ncols <= n or ncols == tn)
    return pl.pallas_call(
        functools.partial(_cast_kernel, valid_cols=min(tn, n - col0)),
        out_shape=jax.ShapeDtypeStruct((k, ncols), BF16),
        grid=(k // tk, ncols // tn),
        in_specs=[pl.BlockSpec((1, tk, tn), lambda i, j: (layer, i, col0 // tn + j))],
        out_specs=pl.BlockSpec((tk, tn), lambda i, j: (i, j)),
        compiler_params=_cparams("parallel", "parallel"),
        name="cast_weight",
    )(w)


def _norm_matmul_kernel(x_ref, g_ref, w_ref, o_ref, hn_ref, *, act):
    @pl.when(pl.program_id(1) == 0)
    def _():
        x = x_ref[...]
        ms = jnp.mean(x * x, axis=-1, keepdims=True)
        hn_ref[...] = (x * lax.rsqrt(ms + EPS) * g_ref[...]).astype(hn_ref.dtype)

    y = jnp.dot(hn_ref[...], w_ref[...], preferred_element_type=F32)
    if act:
        y = jnp.square(jnp.maximum(y, 0.0))
    o_ref[...] = y.astype(o_ref.dtype)


def norm_matmul(x, g, w, *, act=False, out_dtype=F32, tm=1024, tn=2048):
    m, d = x.shape
    n = w.shape[1]
    tm = _pick_tile(m, tm)
    tn = _pick_tile(n, tn)
    return pl.pallas_call(
        functools.partial(_norm_matmul_kernel, act=act),
        out_shape=jax.ShapeDtypeStruct((m, n), out_dtype),
        grid=(m // tm, n // tn),
        in_specs=[pl.BlockSpec((tm, d), lambda i, j: (i, 0)),
                  pl.BlockSpec((1, d), lambda i, j: (0, 0)),
                  pl.BlockSpec((d, tn), lambda i, j: (0, j))],
        out_specs=pl.BlockSpec((tm, tn), lambda i, j: (i, j)),
        scratch_shapes=[pltpu.VMEM((tm, d), BF16)],
        compiler_params=_cparams("parallel", "arbitrary"),
        name="norm_matmul",
    )(x, g.reshape(1, d), w)


def _matmul_res_kernel(a_ref, w_ref, r_ref, o_ref, acc_ref, *, nk):
    k = pl.program_id(2)

    @pl.when(k == 0)
    def _():
        acc_ref[...] = jnp.zeros_like(acc_ref)

    acc_ref[...] += jnp.dot(a_ref[...], w_ref[...], preferred_element_type=F32)

    @pl.when(k == nk - 1)
    def _():
        o_ref[...] = r_ref[...] + acc_ref[...]


def matmul_residual(a, w, res, *, tm=1024, tn=1024, tk=2048):
    m, kdim = a.shape
    n = w.shape[1]
    tm = _pick_tile(m, tm)
    tn = _pick_tile(n, tn)
    tk = _pick_tile(kdim, tk)
    nk = kdim // tk
    return pl.pallas_call(
        functools.partial(_matmul_res_kernel, nk=nk),
        out_shape=jax.ShapeDtypeStruct((m, n), F32),
        grid=(m // tm, n // tn, nk),
        in_specs=[pl.BlockSpec((tm, tk), lambda i, j, k: (i, k)),
                  pl.BlockSpec((tk, tn), lambda i, j, k: (k, j)),
                  pl.BlockSpec((tm, tn), lambda i, j, k: (i, j))],
        out_specs=pl.BlockSpec((tm, tn), lambda i, j, k: (i, j)),
        scratch_shapes=[pltpu.VMEM((tm, tn), F32)],
        compiler_params=_cparams("parallel", "parallel", "arbitrary"),
        name="matmul_residual",
    )(a, w, res)


def _softplus(x):
    return jnp.maximum(x, 0.0) + jnp.log1p(jnp.exp(-jnp.abs(x)))


def _silu(x):
    return x * jax.nn.sigmoid(x)


def _bdot(a, b):
    return jnp.dot(a.astype(BF16), b.astype(BF16), preferred_element_type=F32)


def _bdot_nt(a, b):
    return lax.dot_general(a.astype(BF16), b.astype(BF16), (((1,), (1,)), ((), ())),
                           preferred_element_type=F32)


def _bdot_tn(a, b):
    return lax.dot_general(a.astype(BF16), b.astype(BF16), (((0,), (0,)), ((), ())),
                           preferred_element_type=F32)


def _unit_lower_inverses(lmats):
    c = lmats[0].shape[0]
    row = lax.broadcasted_iota(jnp.int32, (c, c), 0)
    col = lax.broadcasted_iota(jnp.int32, (c, c), 1)

    def same_block(size):
        shift = size.bit_length() - 1
        return (row >> shift) == (col >> shift)

    eye = jnp.where(row == col, 1.0, 0.0)
    n1 = [jnp.where(same_block(8), l, 0.0) for l in lmats]
    n2 = [_bdot(a, a) for a in n1]
    t = [eye - a for a in n1]
    t = [ti + _bdot(ti, b) for ti, b in zip(t, n2)]
    n4 = [_bdot(b, b) for b in n2]
    t = [ti + _bdot(ti, b) for ti, b in zip(t, n4)]
    size = 16
    while size <= c:
        cross = same_block(size) & jnp.logical_not(same_block(size // 2))
        tc = [_bdot(ti, jnp.where(cross, l, 0.0)) for ti, l in zip(t, lmats)]
        t = [ti - _bdot(tci, ti) for ti, tci in zip(t, tc)]
        size *= 2
    return t


def _dn_core_kernel(alog_ref, dtb_ref,
                    q_ref, k_ref, v_ref, z_ref, gate_ref,
                    cwq_ref, cwk_ref, cwv_ref, nw_ref,
                    cbq_ref, cbk_ref, cbv_ref, s0_ref,
                    o_ref, sout_ref,
                    xq_ref, xk_ref, xv_ref, s_ref,
                    *, tb, t_valid, nt, npair):
    hg = pl.program_id(1)
    t = pl.program_id(2)
    c = DN_CHUNK
    nchunk = tb // c
    hd = DN_HEAD
    pad = 8
    tail = CONV_W - 1

    @pl.when(t == 0)
    def _():
        xq_ref[pad - tail:pad, :] = cbq_ref[0]
        xk_ref[pad - tail:pad, :] = cbk_ref[0]
        xv_ref[pad - tail:pad, :] = cbv_ref[0]
        s_ref[...] = s0_ref[0]

    xq_ref[pad:pad + tb, :] = q_ref[0]
    xk_ref[pad:pad + tb, :] = k_ref[0]
    xv_ref[pad:pad + tb, :] = v_ref[0]

    def conv(x_ref, cw_ref):
        acc = x_ref[pad - tail:pad - tail + tb, :] * cw_ref[0:1, :]
        for tap in range(1, CONV_W):
            acc = acc + x_ref[pad - tail + tap:pad - tail + tap + tb, :] * cw_ref[tap:tap + 1, :]
        return _silu(acc)

    q_all = conv(xq_ref, cwq_ref)
    k_all = conv(xk_ref, cwk_ref)
    v_all = conv(xv_ref, cwv_ref)

    xq_ref[pad - tail:pad, :] = xq_ref[pad + tb - tail:pad + tb, :]
    xk_ref[pad - tail:pad, :] = xk_ref[pad + tb - tail:pad + tb, :]
    xv_ref[pad - tail:pad, :] = xv_ref[pad + tb - tail:pad + tb, :]

    qs, ks = [], []
    for p in range(npair):
        qp = q_all[:, p * hd:(p + 1) * hd]
        kp = k_all[:, p * hd:(p + 1) * hd]
        qs.append(qp * lax.rsqrt(jnp.sum(qp * qp, axis=-1, keepdims=True) + EPS) * (hd ** -0.5))
        ks.append(kp * lax.rsqrt(jnp.sum(kp * kp, axis=-1, keepdims=True) + EPS))

    row = lax.broadcasted_iota(jnp.int32, (c, c), 0)
    col = lax.broadcasted_iota(jnp.int32, (c, c), 1)
    eye = row == col
    tril = row >= col
    strict = row > col
    lane_t = lax.broadcasted_iota(jnp.int32, (1, c), 1)

    qk_idx = [(ci, p) for ci in range(nchunk) for p in range(npair)]
    qcs = {(ci, p): qs[p][ci * c:(ci + 1) * c] for ci, p in qk_idx}
    kcs = {(ci, p): ks[p][ci * c:(ci + 1) * c] for ci, p in qk_idx}
    qks = {key: _bdot_nt(jnp.concatenate([qcs[key], kcs[key]], axis=0), kcs[key]) for key in qk_idx}
    systems = [(ci, p, r) for ci in range(nchunk) for p in range(npair) for r in range(2)]
    lmats, a_intras, beta_cols, egcs, kdecs, sdecs = [], [], [], [], [], []
    for ci, p, r in systems:
        head = 2 * (hg * npair + p) + r
        gates = gate_ref[0, p, ci]
        valid = (t * tb + ci * c + lane_t) < t_valid
        beta_row = jnp.where(valid, jax.nn.sigmoid(gates[r:r + 1, :]), 0.0)
        a_scale = -jnp.exp(jnp.full((1, c), alog_ref[head], F32))
        g_row = jnp.where(valid, a_scale * _softplus(gates[2 + r:3 + r, :] + dtb_ref[head]), 0.0)
        gc_col = jnp.sum(jnp.where(tril, g_row, 0.0), axis=1, keepdims=True)
        gc_row = jnp.sum(jnp.where(eye, gc_col, 0.0), axis=0, keepdims=True)
        beta_col = jnp.sum(jnp.where(eye, beta_row, 0.0), axis=1, keepdims=True)
        decay = jnp.exp(jnp.where(tril, gc_col - gc_row, NEG_INF))
        qkt, kkt = qks[ci, p][:c], qks[ci, p][c:]
        lmats.append(jnp.where(strict, kkt * beta_col * decay, 0.0))
        a_intras.append(jnp.where(tril, qkt * decay, 0.0))
        g_last = gc_col[c - 1:c, :]
        beta_cols.append(beta_col)
        egcs.append(jnp.exp(gc_col))
        kdecs.append(kcs[ci, p] * jnp.exp(g_last - gc_col))
        sdecs.append(jnp.exp(g_last))
    tinvs = _unit_lower_inverses(lmats)
    uws = []
    for i, (ci, p, r) in enumerate(systems):
        vc = v_all[ci * c:(ci + 1) * c, (2 * p + r) * hd:(2 * p + r + 1) * hd]
        rhs = jnp.concatenate([vc * beta_cols[i], kcs[ci, p] * (beta_cols[i] * egcs[i])], axis=1)
        uws.append(_bdot(tinvs[i], rhs))

    nh = 2 * npair
    states = [s_ref[h] for h in range(nh)]
    lhs = [jnp.concatenate([uws[i][:, hd:], qcs[ci, p] * egcs[i]], axis=0)
           for i, (ci, p, r) in enumerate(systems)]
    kdts = [kd.T for kd in kdecs]
    outs = []
    for ci in range(nchunk):
        ids = range(ci * nh, (ci + 1) * nh)
        ws_qs = [_bdot(lhs[i], states[i - ci * nh]) for i in ids]
        v_new = [uws[i][:, :hd] - x[:c] for i, x in zip(ids, ws_qs)]
        outs += [x[c:] + _bdot(a_intras[i], vn) for i, x, vn in zip(ids, ws_qs, v_new)]
        states = [states[i - ci * nh] * sdecs[i] + _bdot(kdts[i], vn) for i, vn in zip(ids, v_new)]
    for h in range(nh):
        s_ref[h] = states[h]

    for i, (ci, p, r) in enumerate(systems):
        o = outs[i]
        cols = slice((2 * p + r) * hd, (2 * p + r + 1) * hd)
        zc = z_ref[0, ci * c:(ci + 1) * c, cols]
        o = o * lax.rsqrt(jnp.mean(o * o, axis=-1, keepdims=True) + EPS) * nw_ref[...] * _silu(zc)
        o_ref[0, ci * c:(ci + 1) * c, cols] = o.astype(o_ref.dtype)

    @pl.when(t == nt - 1)
    def _():
        sout_ref[0] = s_ref[...]


def dn_core(proj, gates_t, conv_w, a_log, dt_bias, norm_w, conv_buf, s0, *, t_valid, tb, npair):
    b, t, _ = proj.shape
    nt = t // tb
    ng = DN_QK_HEADS // npair
    wq = DN_HEAD * npair
    wv = 2 * wq
    kern = functools.partial(_dn_core_kernel, tb=tb, t_valid=t_valid, nt=nt, npair=npair)
    smem = pl.BlockSpec(memory_space=pltpu.SMEM)
    k_blk0 = DN_KEY_DIM // wq
    v_blk0 = 2 * DN_KEY_DIM // wv
    z_blk0 = DN_CONV_DIM // wv
    in_specs = [
        smem, smem,
        pl.BlockSpec((1, tb, wq), lambda bi, h, ti: (bi, ti, h)),
        pl.BlockSpec((1, tb, wq), lambda bi, h, ti: (bi, ti, k_blk0 + h)),
        pl.BlockSpec((1, tb, wv), lambda bi, h, ti: (bi, ti, v_blk0 + h)),
        pl.BlockSpec((1, tb, wv), lambda bi, h, ti: (bi, ti, z_blk0 + h)),
        pl.BlockSpec((1, npair, tb // DN_CHUNK, 4, DN_CHUNK), lambda bi, h, ti: (bi, h, ti, 0, 0)),
        pl.BlockSpec((CONV_W, wq), lambda bi, h, ti: (0, h)),
        pl.BlockSpec((CONV_W, wq), lambda bi, h, ti: (0, k_blk0 + h)),
        pl.BlockSpec((CONV_W, wv), lambda bi, h, ti: (0, v_blk0 + h)),
        pl.BlockSpec((1, DN_HEAD), lambda bi, h, ti: (0, 0)),
        pl.BlockSpec((1, CONV_W - 1, wq), lambda bi, h, ti: (bi, 0, h)),
        pl.BlockSpec((1, CONV_W - 1, wq), lambda bi, h, ti: (bi, 0, k_blk0 + h)),
        pl.BlockSpec((1, CONV_W - 1, wv), lambda bi, h, ti: (bi, 0, v_blk0 + h)),
        pl.BlockSpec((1, 2 * npair, DN_HEAD, DN_HEAD), lambda bi, h, ti: (bi, h, 0, 0)),
    ]
    out_specs = [
        pl.BlockSpec((1, tb, wv), lambda bi, h, ti: (bi, ti, h)),
        pl.BlockSpec((1, 2 * npair, DN_HEAD, DN_HEAD), lambda bi, h, ti: (bi, h, 0, 0)),
    ]
    return pl.pallas_call(
        kern,
        out_shape=[jax.ShapeDtypeStruct((b, t, DN_VAL_DIM), BF16),
                   jax.ShapeDtypeStruct((b, DN_V_HEADS, DN_HEAD, DN_HEAD), F32)],
        grid=(b, ng, nt),
        in_specs=in_specs,
        out_specs=out_specs,
        scratch_shapes=[pltpu.VMEM((tb + 8, wq), F32), pltpu.VMEM((tb + 8, wq), F32),
                        pltpu.VMEM((tb + 8, wv), F32), pltpu.VMEM((2 * npair, DN_HEAD, DN_HEAD), F32)],
        compiler_params=_cparams("parallel", "parallel", "arbitrary"),
        name="dn_core",
    )(a_log, dt_bias, proj, proj, proj, proj, gates_t,
      conv_w, conv_w, conv_w, norm_w.reshape(1, DN_HEAD),
      conv_buf, conv_buf, conv_buf, s0)


def deltanet_layer(x, ln_g, w_main, w_gate, conv_w, a_log, dt_bias, norm_w, w_out, conv_buf, s0):
    b, t, d = x.shape
    x2 = x.reshape(b * t, d)
    proj = norm_matmul(x2, ln_g, w_main).reshape(b, t, -1)
    gates = norm_matmul(x2, ln_g, w_gate).reshape(b, t, -1)[:, :, :2 * DN_V_HEADS]
    tail = CONV_W - 1
    if t >= tail:
        new_buf = proj[:, t - tail:, :DN_CONV_DIM]
    else:
        new_buf = jnp.concatenate([conv_buf, proj[:, :, :DN_CONV_DIM]], axis=1)[:, -tail:]
    tp = -(-t // DN_CHUNK) * DN_CHUNK
    if tp != t:
        proj = jnp.pad(proj, ((0, 0), (0, tp - t), (0, 0)))
        gates = jnp.pad(gates, ((0, 0), (0, tp - t), (0, 0)))
    g5 = gates.reshape(b, tp // DN_CHUNK, DN_CHUNK, 2, DN_QK_HEADS, 2)
    gates_t = jnp.transpose(g5, (0, 4, 1, 3, 5, 2)).reshape(b, DN_QK_HEADS, tp // DN_CHUNK, 4, DN_CHUNK)
    tb = _pick_tile(tp, 256)
    o, s_new = dn_core(proj, gates_t, conv_w, a_log, dt_bias, norm_w, conv_buf, s0,
                       t_valid=t, tb=tb, npair=DN_PAIRS_PER_STEP)
    o = o[:, :t].reshape(b * t, DN_VAL_DIM)
    y = matmul_residual(o, w_out, x2)
    return y.reshape(b, t, d), new_buf, s_new


def mlp_layer(x, ln_g, w_up, w_down):
    b, t, d = x.shape
    x2 = x.reshape(b * t, d)
    hid = norm_matmul(x2, ln_g, w_up, act=True, out_dtype=BF16)
    return matmul_residual(hid, w_down, x2).reshape(b, t, d)


def _rel_bias_table_kernel(rbt_ref, o_ref, *, r0, step, ltab):
    rel = (r0 + pl.program_id(0)) + step * lax.broadcasted_iota(jnp.int32, (1, ltab), 1)
    n = jnp.maximum(rel, 0)
    max_exact = REL_BUCKETS // 2
    nf = jnp.maximum(n, max_exact).astype(F32)
    large = max_exact + (jnp.log(nf / max_exact) / math.log(REL_MAX_DIST / max_exact)
                         * (REL_BUCKETS - max_exact)).astype(jnp.int32)
    large = jnp.minimum(large, REL_BUCKETS - 1)
    bucket = jnp.where(n < max_exact, n, large)
    rbt = rbt_ref[...]
    out = jnp.zeros((MB_HEADS, ltab), F32)
    for bkt in range(REL_BUCKETS):
        out = jnp.where(bucket == bkt, rbt[:, bkt:bkt + 1], out)
    o_ref[0] = out


def rel_bias_table(rel_bias, *, r0, step, nrow, ltab):
    return pl.pallas_call(
        functools.partial(_rel_bias_table_kernel, r0=r0, step=step, ltab=ltab),
        out_shape=jax.ShapeDtypeStruct((nrow, MB_HEADS, ltab), F32),
        grid=(nrow,),
        in_specs=[pl.BlockSpec((MB_HEADS, REL_BUCKETS), lambda i: (0, 0))],
        out_specs=pl.BlockSpec((1, MB_HEADS, ltab), lambda i: (i, 0, 0)),
        compiler_params=_cparams("arbitrary"),
        name="rel_bias_table",
    )(rel_bias.T)


def _qk_norm_kernel(q_ref, k_ref, v_ref, qg_ref, kg_ref, kn_ref, qa_ref, ka_ref, va_ref, km_ref,
                    *, v_transposed):
    qg = qg_ref[...]
    kg = kg_ref[...]
    va_ref[0] = (v_ref[0].T if v_transposed else v_ref[0]).astype(va_ref.dtype)
    for h in range(MB_HEADS):
        sl = slice(h * MB_HEAD_DIM, (h + 1) * MB_HEAD_DIM)
        qh = q_ref[0, :, sl]
        kh = k_ref[0, :, sl]
        qn = qh * lax.rsqrt(jnp.mean(qh * qh, axis=-1, keepdims=True) + EPS) * qg
        kn = kh * lax.rsqrt(jnp.mean(kh * kh, axis=-1, keepdims=True) + EPS) * kg
        kn_ref[0, :, sl] = kn
        qa_ref[0, :, sl] = qn.astype(qa_ref.dtype)
        ka_ref[0, :, sl] = kn.astype(ka_ref.dtype)
        km_ref[0, 0, :, sl] = jnp.sum(kn, axis=0, keepdims=True) / MB_BLOCK


def qk_norm(qkv, q_gain, k_gain, *, rows, attn_dtype, v_transposed):
    b, t, _ = qkv.shape
    nb = t // rows
    d = MB_HEADS * MB_HEAD_DIM
    row_spec = [pl.BlockSpec((1, rows, d), functools.partial(lambda bi, i, c: (bi, i, c), c=c))
                for c in range(3)]
    gain_spec = pl.BlockSpec((1, MB_HEAD_DIM), lambda bi, i: (0, 0))
    v_shape, v_spec = (b, t, d), row_spec[0]
    if v_transposed:
        v_shape, v_spec = (b, d, t), pl.BlockSpec((1, d, rows), lambda bi, i: (bi, 0, i))
    return pl.pallas_call(
        functools.partial(_qk_norm_kernel, v_transposed=v_transposed),
        out_shape=[jax.ShapeDtypeStruct((b, t, d), F32)]
        + [jax.ShapeDtypeStruct((b, t, d), attn_dtype)] * 2
        + [jax.ShapeDtypeStruct(v_shape, attn_dtype), jax.ShapeDtypeStruct((b, nb, 1, d), F32)],
        grid=(b, nb),
        in_specs=row_spec + [gain_spec, gain_spec],
        out_specs=[row_spec[0]] * 3 + [v_spec, pl.BlockSpec((1, 1, 1, d), lambda bi, i: (bi, i, 0, 0))],
        compiler_params=_cparams("parallel", "parallel"),
        name="qk_norm",
    )(qkv, qkv, qkv, q_gain.reshape(1, -1), k_gain.reshape(1, -1))


def _block_scores(q, km):
    nb = km.shape[0]
    km = jnp.concatenate([km, jnp.zeros((128 - nb, km.shape[1]), F32)], axis=0)
    return _bdot_nt(q, km)


def _block_ranks(scores, ncand, nb):
    lane = lax.broadcasted_iota(jnp.int32, scores.shape, 1)
    sm = jnp.where(lane < ncand, scores, NEG_INF)
    cnt = jnp.zeros(scores.shape, jnp.int32)
    for m in range(nb):
        col = sm[:, m:m + 1]
        better = (col > sm) | ((col == sm) & (lane > m))
        cnt = cnt + jnp.where(better & (m < ncand), 1, 0)
    return cnt


def _moba_prompt_kernel(tbl_ref, q_ref, k_ref, vt_ref, km_ref, o_ref, bias_ref, wide_ref, pen_ref,
                        *, nblk, nhead):
    b = pl.program_id(1)
    qi = pl.program_id(2)
    blk = MB_BLOCK
    hd = MB_HEAD_DIM
    scale = hd ** -0.5
    heads = range(nhead)

    @pl.when(b == 0)
    def _():
        start = pl.multiple_of(qi * blk, blk)
        for h in heads:
            u = tbl_ref[h, :, pl.ds(start, 2 * blk)]
            wide_ref[...] = jnp.broadcast_to(u, (blk, 2 * blk))
            tile = pltpu.roll(wide_ref[...], blk + 1, 1, stride=1, stride_axis=0)
            bias_ref[h, qi] = tile[:, :blk]

    def cols(h):
        return slice(h * hd, (h + 1) * hd)

    qs = [q_ref[0, :, cols(h)] for h in heads]
    blk_id = lax.broadcasted_iota(jnp.int32, (nblk, blk), 0)
    for h in heads:
        sm = jnp.where(blk_id < qi, _bdot_nt(km_ref[0, :, 0, cols(h)], qs[h]), NEG_INF)
        cnt = jnp.zeros((nblk, blk), jnp.int32)
        for m in range(nblk):
            rowm = sm[m:m + 1, :]
            better = (rowm > sm) | ((rowm == sm) & (blk_id > m))
            cnt = cnt + jnp.where(better & (m < qi), 1, 0)
        pen_ref[h] = jnp.where((blk_id < qi) & (cnt < MB_TOPK), 0.0, NEG_INF)

    key = lax.broadcasted_iota(jnp.int32, (blk, blk), 0)
    qry = lax.broadcasted_iota(jnp.int32, (blk, blk), 1)
    own0 = pl.multiple_of(qi * blk, blk)
    ss = [_bdot_nt(k_ref[0, pl.ds(own0, blk), cols(h)], qs[h]) for h in heads]
    ss = [jnp.where(key <= qry, ss[h] * scale + bias_ref[h, 0], NEG_INF) for h in heads]
    m0 = [jnp.max(x, axis=0, keepdims=True) for x in ss]
    ps = [jnp.exp(x - mm) for x, mm in zip(ss, m0)]
    l0 = [jnp.sum(p, axis=0, keepdims=True) for p in ps]
    acc0 = [_bdot(vt_ref[0, cols(h), pl.ds(own0, blk)], ps[h]) for h in heads]

    def body(j, carry):
        ms, ls, accs = carry
        n0 = 2 * j
        r0 = pl.multiple_of(n0 * blk, 2 * blk)
        d1 = jnp.maximum(qi - n0 - 1, 0)
        ss = [_bdot_nt(k_ref[0, pl.ds(r0, 2 * blk), cols(h)], qs[h]) for h in heads]
        add = [jnp.concatenate([bias_ref[h, qi - n0] + pen_ref[h, pl.ds(n0, 1), :],
                                bias_ref[h, d1] + pen_ref[h, pl.ds(n0 + 1, 1), :]], axis=0) for h in heads]
        ss = [ss[h] * scale + add[h] for h in heads]
        m_new = [jnp.maximum(ms[h], jnp.max(ss[h], axis=0, keepdims=True)) for h in heads]
        alpha = [jnp.exp(ms[h] - m_new[h]) for h in heads]
        ps = [jnp.exp(ss[h] - m_new[h]) for h in heads]
        ls = tuple(ls[h] * alpha[h] + jnp.sum(ps[h], axis=0, keepdims=True) for h in heads)
        pv = [_bdot(vt_ref[0, cols(h), pl.ds(r0, 2 * blk)], ps[h]) for h in heads]
        accs = tuple(accs[h] * alpha[h] + pv[h] for h in heads)
        return tuple(m_new), ls, accs

    ms, ls, accs = lax.fori_loop(0, (qi + 1) // 2, body, (tuple(m0), tuple(l0), tuple(acc0)))
    for h in heads:
        o_ref[0, :, cols(h)] = (accs[h] / ls[h]).T.astype(o_ref.dtype)


def moba_prompt_attention(q, k, vt, kmean, tbl):
    b, s, d = q.shape
    nblk = s // MB_BLOCK
    assert nblk % 2 == 0
    nh = MB_HEADS_PER_STEP
    w = nh * MB_HEAD_DIM
    return pl.pallas_call(
        functools.partial(_moba_prompt_kernel, nblk=nblk, nhead=nh),
        out_shape=jax.ShapeDtypeStruct((b, s, d), BF16),
        grid=(MB_HEADS // nh, b, nblk),
        in_specs=[pl.BlockSpec((nh, 1, s + MB_BLOCK), lambda h, bi, i: (h, 0, 0)),
                  pl.BlockSpec((1, MB_BLOCK, w), lambda h, bi, i: (bi, i, h)),
                  pl.BlockSpec((1, s, w), lambda h, bi, i: (bi, 0, h)),
                  pl.BlockSpec((1, w, s), lambda h, bi, i: (bi, h, 0)),
                  pl.BlockSpec((1, nblk, 1, w), lambda h, bi, i: (bi, 0, 0, h))],
        out_specs=pl.BlockSpec((1, MB_BLOCK, w), lambda h, bi, i: (bi, i, h)),
        scratch_shapes=[pltpu.VMEM((nh, nblk, MB_BLOCK, MB_BLOCK), F32),
                        pltpu.VMEM((MB_BLOCK, 2 * MB_BLOCK), F32),
                        pltpu.VMEM((nh, nblk, MB_BLOCK), F32)],
        compiler_params=_cparams("parallel", "arbitrary", "arbitrary"),
        name="moba_prompt_attention",
    )(tbl, q, k, vt, kmean)


def moba_prompt_layer(x, ln_g, w_qkv, q_gain, k_gain, w_out, tbl):
    b, s, d = x.shape
    assert s % MB_BLOCK == 0
    x2 = x.reshape(b * s, d)
    qkv = norm_matmul(x2, ln_g, w_qkv).reshape(b, s, -1)
    kn, qa, ka, vt, kmean = qk_norm(qkv, q_gain, k_gain, rows=MB_BLOCK, attn_dtype=BF16, v_transposed=True)
    o = moba_prompt_attention(qa, ka, vt, kmean, tbl)
    y = matmul_residual(o.reshape(b * s, d), w_out, x2).reshape(b, s, d)
    new_k = kn.reshape(b, s, MB_HEADS, MB_HEAD_DIM)
    new_v = qkv[:, :, 2 * d:].reshape(b, s, MB_HEADS, MB_HEAD_DIM)
    return y, new_k, new_v


PAGES_PER_MEAN_STEP = 4


def _page_block_mean_kernel(pt_ref, *refs):
    del pt_ref
    o_ref = refs[-1]
    for blk in range(PAGES_PER_MEAN_STEP // 2):
        a_ref, b_ref = refs[2 * blk], refs[2 * blk + 1]
        o_ref[0, blk] = (jnp.sum(a_ref[0, 0], axis=0) + jnp.sum(b_ref[0, 0], axis=0)) / MB_BLOCK


def page_block_means(cache_k, layer, pt_flat, bsz, n_pages):
    nb = n_pages // 2
    npg = PAGES_PER_MEAN_STEP
    assert n_pages % npg == 0
    page_shape = cache_k.shape[2:]

    def page_spec(off):
        return pl.BlockSpec((1, 1) + page_shape,
                            lambda bi, n, pt: (layer, pt[bi * n_pages + npg * n + off], 0, 0, 0))

    return pl.pallas_call(
        _page_block_mean_kernel,
        out_shape=jax.ShapeDtypeStruct((bsz, nb) + page_shape[1:], F32),
        grid_spec=pltpu.PrefetchScalarGridSpec(
            num_scalar_prefetch=1, grid=(bsz, n_pages // npg),
            in_specs=[page_spec(off) for off in range(npg)],
            out_specs=pl.BlockSpec((1, npg // 2) + page_shape[1:], lambda bi, n, pt: (bi, n, 0, 0))),
        compiler_params=_cparams("parallel", "parallel"),
        name="page_block_means",
    )(pt_flat, *([cache_k] * npg))


def _sample_select_kernel(q_ref, km_ref, o_ref, *, past, nb):
    lane = lax.broadcasted_iota(jnp.int32, (8, 128), 1)
    ncand = (past + lax.broadcasted_iota(jnp.int32, (8, 1), 0)) // MB_BLOCK
    cand = lane < ncand
    t = q_ref.shape[1]
    for h in range(MB_HEADS):
        sl = slice(h * MB_HEAD_DIM, (h + 1) * MB_HEAD_DIM)
        q = jnp.concatenate([q_ref[0, :, sl], jnp.zeros((8 - t, MB_HEAD_DIM), F32)], axis=0)
        rank = _block_ranks(_block_scores(q, km_ref[0, :, 0, sl]), ncand, nb)
        out = jnp.zeros((8, 128), jnp.int32)
        for r in range(MB_TOPK):
            idx = jnp.sum(jnp.where(cand & (rank == r), lane, 0), axis=1, keepdims=True)
            out = jnp.where(lane == r, idx, out)
        o_ref[0, h] = out


def sample_select(qn, kmean, past):
    b, t, d = qn.shape
    nb = kmean.shape[1]
    assert t <= 8 and nb <= 128
    return pl.pallas_call(
        functools.partial(_sample_select_kernel, past=past, nb=nb),
        out_shape=jax.ShapeDtypeStruct((b, MB_HEADS, 8, 128), jnp.int32),
        grid=(b,),
        in_specs=[pl.BlockSpec((1, t, d), lambda bi: (bi, 0, 0)),
                  pl.BlockSpec((1, nb, 1, d), lambda bi: (bi, 0, 0, 0))],
        out_specs=pl.BlockSpec((1, MB_HEADS, 8, 128), lambda bi: (bi, 0, 0, 0)),
        compiler_params=_cparams("parallel"),
        name="sample_select",
    )(qn, kmean)


def _sample_attn_kernel(idx_ref, pt_ref, q_ref, kn_ref, vn_ref, tbl_ref, ck_ref, cv_ref,
                        o_ref, kbuf, vbuf, sem, *, layer, n_pages, nq):
    g = pl.program_id(0)
    nsteps = pl.num_programs(0)
    slot = g % 2
    per_q = MB_TOPK * 2
    past = n_pages * PAGE_SIZE
    scale = MB_HEAD_DIM ** -0.5

    def logical_page(gg, t, r, pg):
        return idx_ref[(gg * nq + t) * MB_TOPK + r] * 2 + pg

    def page_copies(gg, sl):
        b = gg // MB_HEADS
        h = gg % MB_HEADS
        out = []
        for t in range(nq):
            for r in range(MB_TOPK):
                for pg in range(2):
                    i = (t * MB_TOPK + r) * 2 + pg
                    phys = pt_ref[b * n_pages + logical_page(gg, t, r, pg)]
                    out.append(pltpu.make_async_copy(ck_ref.at[layer, phys, :, h, :], kbuf.at[sl, i], sem.at[sl, 0]))
                    out.append(pltpu.make_async_copy(cv_ref.at[layer, phys, :, h, :], vbuf.at[sl, i], sem.at[sl, 1]))
        return out

    @pl.when(g == 0)
    def _():
        for cp in page_copies(g, slot):
            cp.start()

    @pl.when(g + 1 < nsteps)
    def _():
        for cp in page_copies(g + 1, 1 - slot):
            cp.start()

    for cp in page_copies(g, slot):
        cp.wait()

    zpad = jnp.zeros((PAGE_SIZE - nq, MB_HEAD_DIM), F32)
    kown = jnp.concatenate([kn_ref[0], zpad], axis=0)
    vown = jnp.concatenate([vn_ref[0], zpad], axis=0)
    lane = lax.broadcasted_iota(jnp.int32, (1, PAGE_SIZE), 1)
    own_bias0 = pl.multiple_of(n_pages * PAGE_SIZE, PAGE_SIZE)
    npast = per_q * PAGE_SIZE
    queries = range(nq)
    q8s = [jnp.broadcast_to(q_ref[0, t:t + 1, :], (8, MB_HEAD_DIM)) for t in queries]
    kts = [kbuf[slot, t * per_q:(t + 1) * per_q].reshape(npast, MB_HEAD_DIM) for t in queries]
    vts = [vbuf[slot, t * per_q:(t + 1) * per_q].reshape(npast, MB_HEAD_DIM) for t in queries]
    biases, okms = [], []
    for t in queries:
        bias, oks = [], []
        for r in range(MB_TOPK):
            ok = r < min((past + t) // MB_BLOCK, MB_TOPK)
            for pg in range(2):
                col0 = pl.multiple_of(logical_page(g, t, r, pg) * PAGE_SIZE, PAGE_SIZE)
                bias.append(tbl_ref[t, 0, :, pl.ds(col0, PAGE_SIZE)])
                oks.append(jnp.full((1, PAGE_SIZE), ok))
        bias.append(tbl_ref[t, 0, :, pl.ds(own_bias0, PAGE_SIZE)])
        oks.append(lane <= t)
        biases.append(jnp.concatenate(bias, axis=1))
        okms.append(jnp.concatenate(oks, axis=1))
    s_past = [_bdot_nt(q8s[t], kts[t])[0:1] for t in queries]
    s_own = [_bdot_nt(q8s[t], kown)[0:1] for t in queries]
    scs = [jnp.where(okms[t], jnp.concatenate([s_past[t], s_own[t]], axis=1) * scale + biases[t], NEG_INF)
           for t in queries]
    ps = [jnp.exp(sc - jnp.max(sc, axis=1, keepdims=True)) for sc in scs]
    ls = [jnp.sum(p, axis=1, keepdims=True) for p in ps]
    p8s = [jnp.broadcast_to(p, (8, p.shape[1])) for p in ps]
    acc_past = [_bdot(p8s[t][:, :npast], vts[t])[0:1] for t in queries]
    acc_own = [_bdot(p8s[t][:, npast:], vown)[0:1] for t in queries]
    for t in queries:
        o_ref[0, t:t + 1, :] = (acc_past[t] + acc_own[t]) / ls[t]


def sample_attention(qn, kn, vn, cache_k, cache_v, layer, idx_flat, pt_flat, tbl, n_pages):
    b, t, d = qn.shape
    hd = MB_HEAD_DIM
    ngather = t * MB_TOPK * 2
    row_spec = pl.BlockSpec((1, t, hd), lambda g, idx, pt: (g // MB_HEADS, 0, g % MB_HEADS))
    in_specs = [
        row_spec, row_spec, row_spec,
        pl.BlockSpec((t, 1, 1, tbl.shape[-1]), lambda g, idx, pt: (0, g % MB_HEADS, 0, 0)),
        pl.BlockSpec(memory_space=pl.ANY), pl.BlockSpec(memory_space=pl.ANY),
    ]
    return pl.pallas_call(
        functools.partial(_sample_attn_kernel, layer=layer, n_pages=n_pages, nq=t),
        out_shape=jax.ShapeDtypeStruct((b, t, d), F32),
        grid_spec=pltpu.PrefetchScalarGridSpec(
            num_scalar_prefetch=2, grid=(b * MB_HEADS,),
            in_specs=in_specs,
            out_specs=row_spec,
            scratch_shapes=[pltpu.VMEM((2, ngather, PAGE_SIZE, hd), F32),
                            pltpu.VMEM((2, ngather, PAGE_SIZE, hd), F32),
                            pltpu.SemaphoreType.DMA((2, 2))]),
        compiler_params=_cparams("arbitrary"),
        name="sample_attention",
    )(idx_flat, pt_flat, qn, kn, vn, tbl, cache_k, cache_v)


def moba_sample_layer(x, ln_g, w_qkv, q_gain, k_gain, w_out, cache_k, cache_v, layer, page_table, tbl):
    b, t, d = x.shape
    n_pages = page_table.shape[1]
    past = n_pages * PAGE_SIZE
    assert n_pages % 2 == 0 and t <= 8
    x2 = x.reshape(b * t, d)
    qkv = norm_matmul(x2, ln_g, w_qkv).reshape(b, t, -1)
    kn, qn, _, vn, _ = qk_norm(qkv, q_gain, k_gain, rows=t, attn_dtype=F32, v_transposed=False)
    pt_flat = page_table.reshape(-1)
    kmean = page_block_means(cache_k, layer, pt_flat, b, n_pages).reshape(b, n_pages // 2, 1, d)
    idx = sample_select(qn, kmean, past)[:, :, :t, :MB_TOPK].reshape(-1)
    o = sample_attention(qn, kn, vn, cache_k, cache_v, layer, idx, pt_flat, tbl, n_pages)
    y = matmul_residual(o.reshape(b * t, d).astype(BF16), w_out, x2).reshape(b, t, d)
    new_k = kn.reshape(b, t, MB_HEADS, MB_HEAD_DIM)
    new_v = qkv[:, :, 2 * d:].reshape(b, t, MB_HEADS, MB_HEAD_DIM)
    return y, new_k, new_v


def kernel(x_prompt, x_sample, state_delta, state_conv, cache_k, cache_v, page_table, rel_bias,
           ln_mix, ln_mlp, dn_w_in, dn_conv_w, dn_a_log, dn_dt_bias, dn_norm_w, dn_w_out,
           mb_w_qkv, mb_q_norm, mb_k_norm, mb_w_out, mlp_w_up, mlp_w_down):
    depth = ln_mix.shape[0]
    bp, sp, _ = x_prompt.shape
    bd, td, _ = x_sample.shape
    n_pages = page_table.shape[1]
    past = n_pages * PAGE_SIZE

    tbl_p = rel_bias_table(rel_bias, r0=1 - MB_BLOCK, step=1, nrow=1, ltab=sp + MB_BLOCK)
    tbl_p = tbl_p.reshape(MB_HEADS, 1, sp + MB_BLOCK)
    tbl_s = rel_bias_table(rel_bias, r0=past, step=-1, nrow=td, ltab=past + PAGE_SIZE)
    tbl_s = tbl_s.reshape(td, MB_HEADS, 1, past + PAGE_SIZE)

    xp, xd = x_prompt, x_sample
    sdp, scp, sds, scs, kps, vps, kds, vds = [], [], [], [], [], [], [], []
    for i in range(depth):
        j = i // 2
        if i % 2 == 0:
            w_main = cast_weight(dn_w_in, j, 0, DN_CONV_DIM + DN_VAL_DIM)
            w_gate = cast_weight(dn_w_in, j, DN_CONV_DIM + DN_VAL_DIM, 128)
            w_out = cast_weight(dn_w_out, j)
            args = (w_main, w_gate, dn_conv_w[j], dn_a_log[j], dn_dt_bias[j], dn_norm_w[j], w_out)
            buf0 = jnp.zeros((bp, CONV_W - 1, DN_CONV_DIM), F32)
            s0 = jnp.zeros((bp, DN_V_HEADS, DN_HEAD, DN_HEAD), F32)
            xp, buf_p, s_p = deltanet_layer(xp, ln_mix[i], *args, buf0, s0)
            xd, buf_d, s_d = deltanet_layer(xd, ln_mix[i], *args, state_conv[j], state_delta[j])
            sdp.append(s_p)
            scp.append(buf_p)
            sds.append(s_d)
            scs.append(buf_d)
        else:
            w_qkv = cast_weight(mb_w_qkv, j)
            w_out = cast_weight(mb_w_out, j)
            xp, kp, vp = moba_prompt_layer(xp, ln_mix[i], w_qkv, mb_q_norm[j], mb_k_norm[j], w_out, tbl_p)
            xd, kd, vd = moba_sample_layer(xd, ln_mix[i], w_qkv, mb_q_norm[j], mb_k_norm[j], w_out,
                                           cache_k, cache_v, j, page_table, tbl_s)
            kps.append(kp)
            vps.append(vp)
            kds.append(kd)
            vds.append(vd)
        w_up = cast_weight(mlp_w_up, i)
        w_down = cast_weight(mlp_w_down, i)
        xp = mlp_layer(xp, ln_mlp[i], w_up, w_down)
        xd = mlp_layer(xd, ln_mlp[i], w_up, w_down)
    return (xp, xd, jnp.stack(sdp), jnp.stack(scp), jnp.stack(kps), jnp.stack(vps),
            jnp.stack(sds), jnp.stack(scs), jnp.stack(kds), jnp.stack(vds))
```
